```python
import math
import jax, jax.numpy as jnp
from jax import lax
import numpy as np

D_MODEL = 2048
BATCH = 1
SEQ = 8192
DEPTH = 2

ATTN_HEADS = 8
ATTN_HEAD_DIM = 128
ATTN_WIDTH = ATTN_HEADS * ATTN_HEAD_DIM
MOBA_BLOCK = 256
MOBA_TOPK = 3
Q_BLOCK = 128
REL_BUCKETS = 32
REL_MAX_DIST = 128
SSD_HEADS = 32
SSD_HEAD_DIM = 64
SSD_WIDTH = SSD_HEADS * SSD_HEAD_DIM
SSD_GROUPS = 4
SSD_STATE = 128
SSD_CONV = 4
SSD_CHUNK = 256
SSD_XBC = SSD_WIDTH + 2 * SSD_GROUPS * SSD_STATE
CONV_WIDTH = 1024
CONV_K = 3
FFN_HIDDEN = 5632
FFN_RESIDUAL = 0.5
N_SUBLAYERS = 3
SEQ_ALIGN = math.lcm(MOBA_BLOCK, SSD_CHUNK)
MIX_SIZES = (ATTN_WIDTH, ATTN_WIDTH, ATTN_WIDTH,
             SSD_WIDTH, SSD_XBC, SSD_HEADS,
             CONV_WIDTH, CONV_WIDTH, CONV_WIDTH,
             D_MODEL, D_MODEL, D_MODEL)
MIX_IN = sum(MIX_SIZES)
NORM_EPS = 1e-6
NEG_INF = -1e30

kernel_name = "hybrid_moba_ssd_shortconv_block"


def rms_norm(x, gain):
    xf = x.astype(jnp.float32)
    y = xf * lax.rsqrt(jnp.mean(xf * xf, axis=-1, keepdims=True) + NORM_EPS)
    return (y * gain.astype(jnp.float32)).astype(x.dtype)


def causal_depthwise_conv(x, w):
    width = w.shape[0]
    return lax.conv_general_dilated(
        x, w[:, None, :].astype(x.dtype), window_strides=(1,), padding=[(width - 1, 0)],
        dimension_numbers=('NWC', 'WIO', 'NWC'), feature_group_count=x.shape[-1])


def swiglu_ffn(h, w_in, w_out):
    gate, up = jnp.split(h @ w_in, 2, axis=-1)
    return (jax.nn.silu(gate) * up) @ w_out


def t5_causal_bucket(dist):
    n = jnp.maximum(dist, 0)
    max_exact = REL_BUCKETS // 2
    ratio = jnp.log(jnp.maximum(n, 1).astype(jnp.float32) / max_exact) / math.log(REL_MAX_DIST / max_exact)
    large = max_exact + (ratio * (REL_BUCKETS - max_exact)).astype(jnp.int32)
    large = jnp.minimum(large, REL_BUCKETS - 1)
    return jnp.where(n < max_exact, n, large)


def moba_attention(q, k, v, rel_bias):
    b, h, s, d = q.shape
    nb = s // MOBA_BLOCK
    n_sel = min(MOBA_TOPK, nb)
    nq = s // Q_BLOCK
    scale = d ** -0.5
    kb = k.reshape(b, h, nb, MOBA_BLOCK, d)
    vb = v.reshape(b, h, nb, MOBA_BLOCK, d)
    k_mean = jnp.mean(kb.astype(jnp.float32), axis=3).astype(k.dtype)
    q_blk = jnp.arange(s) // MOBA_BLOCK
    gate = jnp.einsum('bhsd,bhnd->bhsn', q, k_mean).astype(jnp.float32)
    past = jnp.arange(nb)[None, :] < q_blk[:, None]
    gate = jnp.where(past, gate, -jnp.inf)
    _, sel = lax.top_k(gate, n_sel)
    sel_valid = sel < q_blk[:, None]
    q_c = q.reshape(b, h, nq, Q_BLOCK, d).transpose(2, 0, 1, 3, 4)
    sel_c = sel.reshape(b, h, nq, Q_BLOCK, n_sel).transpose(2, 0, 1, 3, 4)
    valid_c = sel_valid.reshape(b, h, nq, Q_BLOCK, n_sel).transpose(2, 0, 1, 3, 4)
    table = rel_bias.T
    head_idx = jnp.arange(h)[None, :, None, None, None]
    gather_blocks = jax.vmap(jax.vmap(lambda blocks, idx: blocks[idx]))

    def one_query_block(args):
        ci, qb, selb, validb = args
        q_pos = ci * Q_BLOCK + jnp.arange(Q_BLOCK)
        own = (ci * Q_BLOCK) // MOBA_BLOCK
        own_pos = own * MOBA_BLOCK + jnp.arange(MOBA_BLOCK)
        k_own = lax.dynamic_index_in_dim(kb, own, axis=2, keepdims=False)
        v_own = lax.dynamic_index_in_dim(vb, own, axis=2, keepdims=False)
        k_sel = gather_blocks(kb, selb)
        v_sel = gather_blocks(vb, selb)
        sel_pos = selb[..., None] * MOBA_BLOCK + jnp.arange(MOBA_BLOCK)
        bias_sel = table[head_idx, t5_causal_bucket(q_pos[:, None, None] - sel_pos)]
        l_sel = jnp.einsum('bhqd,bhqjkd->bhqjk', qb, k_sel).astype(jnp.float32) * scale + bias_sel
        l_sel = jnp.where(validb[..., None], l_sel, NEG_INF)
        bias_own = table[:, t5_causal_bucket(q_pos[:, None] - own_pos[None, :])]
        l_own = jnp.einsum('bhqd,bhkd->bhqk', qb, k_own).astype(jnp.float32) * scale + bias_own
        l_own = jnp.where(own_pos[None, :] <= q_pos[:, None], l_own, NEG_INF)
        logits = jnp.concatenate([l_sel.reshape(b, h, Q_BLOCK, n_sel * MOBA_BLOCK), l_own], axis=-1)
        p = jax.nn.softmax(logits, axis=-1)
        p_sel = p[..., :n_sel * MOBA_BLOCK].reshape(b, h, Q_BLOCK, n_sel, MOBA_BLOCK).astype(v.dtype)
        p_own = p[..., n_sel * MOBA_BLOCK:].astype(v.dtype)
        return (jnp.einsum('bhqjk,bhqjkd->bhqd', p_sel, v_sel)
                + jnp.einsum('bhqk,bhkd->bhqd', p_own, v_own))

    out = lax.map(one_query_block, (jnp.arange(nq), q_c, sel_c, valid_c))
    return out.transpose(1, 2, 0, 3, 4).reshape(b, h, s, d)


def ssd_chunked_scan(xs, dt, a, bm, cm):
    b, s = xs.shape[0], xs.shape[1]
    nc = s // SSD_CHUNK
    r = SSD_HEADS // SSD_GROUPS
    f32 = jnp.float32
    xd = (xs * dt[..., None]).reshape(b, nc, SSD_CHUNK, SSD_GROUPS, r, SSD_HEAD_DIM)
    ad = (dt.astype(f32) * a.astype(f32)).reshape(b, nc, SSD_CHUNK, SSD_GROUPS, r).transpose(0, 3, 4, 1, 2)
    bc = bm.reshape(b, nc, SSD_CHUNK, SSD_GROUPS, SSD_STATE)
    cc = cm.reshape(b, nc, SSD_CHUNK, SSD_GROUPS, SSD_STATE)
    a_cs = jnp.cumsum(ad, axis=-1)
    causal = jnp.tril(jnp.ones((SSD_CHUNK, SSD_CHUNK), dtype=bool))
    decay = jnp.exp(jnp.where(causal, a_cs[..., :, None] - a_cs[..., None, :], -jnp.inf))
    cb = jnp.einsum('bclgn,bcsgn->bgcls', cc, bc)
    y_diag = jnp.einsum('bgrcls,bcsgrp->bclgrp', cb[:, :, None] * decay, xd)
    decay_to_end = jnp.exp(a_cs[..., -1:] - a_cs)
    states = jnp.einsum('bclgn,bgrcl,bclgrp->cbgrpn', bc, decay_to_end, xd)
    chunk_decay = jnp.exp(a_cs[..., -1]).transpose(3, 0, 1, 2)

    def carry_state(hstate, inp):
        st, dec = inp
        return dec[..., None, None] * hstate + st, hstate

    h0 = jnp.zeros(states.shape[1:], states.dtype)
    _, prev = lax.scan(carry_state, h0, (states, chunk_decay))
    y_off = jnp.einsum('bclgn,cbgrpn,bgrcl->bclgrp', cc, prev, jnp.exp(a_cs))
    return (y_diag + y_off).reshape(b, s, SSD_HEADS, SSD_HEAD_DIM).astype(xs.dtype)


def token_mix(h, w_mix_in, qk_norm, rel_bias, w_ssd_conv, b_ssd_conv, ssd_dt_bias, ssd_a_log,
              ssd_d, ssd_norm, w_sc_conv, w_br_attn, w_br_ssd, w_br_conv, w_mix_out):
    b, s, _ = h.shape
    s_pad = -(-s // SEQ_ALIGN) * SEQ_ALIGN
    hp = jnp.pad(h, ((0, 0), (0, s_pad - s), (0, 0)))
    splits = [int(v) for v in np.cumsum(MIX_SIZES)[:-1]]
    (q, k, v, z, xbc, dt, conv_b, conv_c, conv_x,
     g_attn, g_ssd, g_conv) = jnp.split(hp @ w_mix_in, splits, axis=-1)

    def heads(t):
        return t.reshape(b, s_pad, ATTN_HEADS, ATTN_HEAD_DIM)
    qh = rms_norm(heads(q), qk_norm[0]).transpose(0, 2, 1, 3)
    kh = rms_norm(heads(k), qk_norm[1]).transpose(0, 2, 1, 3)
    vh = heads(v).transpose(0, 2, 1, 3)
    y_attn = moba_attention(qh, kh, vh, rel_bias).transpose(0, 2, 1, 3).reshape(b, s_pad, ATTN_WIDTH)

    xbc = jax.nn.silu(causal_depthwise_conv(xbc, w_ssd_conv) + b_ssd_conv)
    xs, bm, cm = jnp.split(xbc, [SSD_WIDTH, SSD_WIDTH + SSD_GROUPS * SSD_STATE], axis=-1)
    xs = xs.reshape(b, s_pad, SSD_HEADS, SSD_HEAD_DIM)
    dt = jax.nn.softplus(dt + ssd_dt_bias)
    a = -jnp.exp(ssd_a_log)
    y = ssd_chunked_scan(xs, dt, a, bm.reshape(b, s_pad, SSD_GROUPS, SSD_STATE),
                         cm.reshape(b, s_pad, SSD_GROUPS, SSD_STATE))
    y = (y + xs * ssd_d[:, None]).reshape(b, s_pad, SSD_WIDTH) * jax.nn.silu(z)
    group_w = SSD_WIDTH // SSD_GROUPS
    y_ssd = rms_norm(y.reshape(b, s_pad, SSD_GROUPS, group_w),
                     ssd_norm.reshape(SSD_GROUPS, group_w)).reshape(b, s_pad, SSD_WIDTH)

    y_conv = conv_b * causal_depthwise_conv(conv_c * conv_x, w_sc_conv)

    merged = (jax.nn.sigmoid(g_attn) * (y_attn @ w_br_attn)
              + jax.nn.sigmoid(g_ssd) * (y_ssd @ w_br_ssd)
              + jax.nn.sigmoid(g_conv) * (y_conv @ w_br_conv))
    return (merged @ w_mix_out)[:, :s]


def setup_inputs(seed: int = 0) -> dict:
    key = jax.random.key(seed)
    ks = jax.random.split(key, 24)
    f32 = jnp.float32

    def dense(k, shape, fan_in):
        return jax.random.normal(k, shape, f32) * fan_in ** -0.5

    def near_one(k, shape):
        return 1.0 + 0.1 * jax.random.normal(k, shape, f32)

    dt0 = jnp.exp(jax.random.uniform(ks[12], (DEPTH, SSD_HEADS), f32, math.log(1e-3), math.log(1e-1)))
    return {
        'x': jax.random.normal(ks[0], (BATCH, SEQ, D_MODEL), f32),
        'c': jax.random.normal(ks[1], (BATCH, D_MODEL), f32),
        'w_ada': dense(ks[2], (DEPTH, D_MODEL, N_SUBLAYERS * 3 * D_MODEL), D_MODEL),
        'b_ada': 0.02 * jax.random.normal(ks[3], (DEPTH, N_SUBLAYERS * 3 * D_MODEL), f32),
        'norm_gain': near_one(ks[4], (DEPTH, N_SUBLAYERS, D_MODEL)),
        'w_ffn_in': dense(ks[5], (DEPTH, 2, D_MODEL, 2 * FFN_HIDDEN), D_MODEL),
        'w_ffn_out': dense(ks[6], (DEPTH, 2, FFN_HIDDEN, D_MODEL), FFN_HIDDEN),
        'w_mix_in': dense(ks[7], (DEPTH, D_MODEL, MIX_IN), D_MODEL),
        'qk_norm': near_one(ks[8], (DEPTH, 2, ATTN_HEAD_DIM)),
        'rel_bias': 0.5 * jax.random.normal(ks[9], (REL_BUCKETS, ATTN_HEADS), f32),
        'w_ssd_conv': dense(ks[10], (DEPTH, SSD_CONV, SSD_XBC), SSD_CONV),
        'b_ssd_conv': 0.02 * jax.random.normal(ks[11], (DEPTH, SSD_XBC), f32),
        'ssd_dt_bias': dt0 + jnp.log(-jnp.expm1(-dt0)),
        'ssd_a_log': jnp.log(jax.random.uniform(ks[13], (DEPTH, SSD_HEADS), f32, 1.0, 16.0)),
        'ssd_d': near_one(ks[14], (DEPTH, SSD_HEADS)),
        'ssd_norm': near_one(ks[15], (DEPTH, SSD_WIDTH)),
        'w_sc_conv': dense(ks[16], (DEPTH, CONV_K, CONV_WIDTH), CONV_K),
        'w_br_attn': dense(ks[17], (DEPTH, ATTN_WIDTH, D_MODEL), ATTN_WIDTH),
        'w_br_ssd': dense(ks[18], (DEPTH, SSD_WIDTH, D_MODEL), SSD_WIDTH),
        'w_br_conv': dense(ks[19], (DEPTH, CONV_WIDTH, D_MODEL), CONV_WIDTH),
        'w_mix_out': dense(ks[20], (DEPTH, D_MODEL, D_MODEL), D_MODEL),
    }


def reference(x, c, w_ada, b_ada, norm_gain, w_ffn_in, w_ffn_out, w_mix_in, qk_norm, rel_bias,
              w_ssd_conv, b_ssd_conv, ssd_dt_bias, ssd_a_log, ssd_d, ssd_norm, w_sc_conv,
              w_br_attn, w_br_ssd, w_br_conv, w_mix_out):
    b = x.shape[0]
    cond = jax.nn.silu(c)
    for l in range(DEPTH):
        ada = (cond @ w_ada[l] + b_ada[l]).reshape(b, N_SUBLAYERS, 3, D_MODEL)
        shift, scale, gate = ada[:, :, 0, None], ada[:, :, 1, None], ada[:, :, 2, None]

        def modulated(t, i):
            return rms_norm(t, norm_gain[l, i]) * (1.0 + scale[:, i]) + shift[:, i]

        h = modulated(x, 0)
        x = x + FFN_RESIDUAL * gate[:, 0] * swiglu_ffn(h, w_ffn_in[l, 0], w_ffn_out[l, 0])
        h = modulated(x, 1)
        x = x + gate[:, 1] * token_mix(h, w_mix_in[l], qk_norm[l], rel_bias, w_ssd_conv[l], b_ssd_conv[l],
                                       ssd_dt_bias[l], ssd_a_log[l], ssd_d[l], ssd_norm[l], w_sc_conv[l],
                                       w_br_attn[l], w_br_ssd[l], w_br_conv[l], w_mix_out[l])
        h = modulated(x, 2)
        x = x + FFN_RESIDUAL * gate[:, 2] * swiglu_ffn(h, w_ffn_in[l, 1], w_ffn_out[l, 1])
    return x
```

```python
import functools
import math

import jax
import jax.numpy as jnp
from jax import lax
from jax.experimental import pallas as pl
from jax.experimental.pallas import tpu as pltpu

F32 = jnp.float32
BF16 = jnp.bfloat16

ATTN_HEADS = 8
ATTN_HEAD_DIM = 128
ATTN_WIDTH = ATTN_HEADS * ATTN_HEAD_DIM
MOBA_BLOCK = 256
MOBA_TOPK = 3
REL_BUCKETS = 32
REL_MAX_DIST = 128
SSD_HEADS = 32
SSD_HEAD_DIM = 64
SSD_WIDTH = SSD_HEADS * SSD_HEAD_DIM
SSD_GROUPS = 4
SSD_STATE = 128
SSD_CONV = 4
SSD_CHUNK = 256
SSD_BC = SSD_GROUPS * SSD_STATE
SSD_XBC = SSD_WIDTH + 2 * SSD_BC
CONV_WIDTH = 1024
CONV_K = 3
N_SUBLAYERS = 3
FFN_RESIDUAL = 0.5
NORM_EPS = 1e-6
NEG_INF = -1e30

LANES = 128
VMEM_LIMIT_BYTES = 58 * 1024 * 1024


def _cparams(semantics):
    return pltpu.CompilerParams(dimension_semantics=semantics, vmem_limit_bytes=VMEM_LIMIT_BYTES)


def _sigmoid(x):
    return 1.0 / (1.0 + jnp.exp(-x))


def _silu(x):
    return x * _sigmoid(x)


def _dot(a, b):
    return jnp.dot(a, b, preferred_element_type=F32)


def _dot_nt(a, b):
    return lax.dot_general(a, b, (((1,), (1,)), ((), ())), preferred_element_type=F32)


def _wspec(lead, block, index_map):
    return pl.BlockSpec((None,) * len(lead) + block, lambda i, j: lead + index_map(i, j))


def _mod_norm(x, gain, scale, shift):
    y = x * lax.rsqrt(jnp.mean(x * x, axis=-1, keepdims=True) + NORM_EPS)
    return (y * gain) * (1.0 + scale) + shift


def _ada_kernel(c_ref, w_ref, b_ref, o_ref):
    cond = _silu(c_ref[...]).astype(BF16)
    o_ref[...] = _dot(cond, w_ref[...].astype(BF16)) + b_ref[...]


def _ada_proj(c, w_ada, b_ada, tn=1024):
    depth, d, n = w_ada.shape
    c8 = jnp.broadcast_to(c, (8, d))
    out = pl.pallas_call(
        _ada_kernel,
        grid=(depth, n // tn),
        in_specs=[
            pl.BlockSpec((8, d), lambda l, j: (0, 0)),
            pl.BlockSpec((None, d, tn), lambda l, j: (l, 0, j)),
            pl.BlockSpec((None, 1, tn), lambda l, j: (l, 0, j)),
        ],
        out_specs=pl.BlockSpec((None, 8, tn), lambda l, j: (l, 0, j)),
        out_shape=jax.ShapeDtypeStruct((depth, 8, n), F32),
        compiler_params=_cparams(("arbitrary", "arbitrary")),
        name="ada_proj",
    )(c8, w_ada, b_ada.reshape(depth, 1, n))
    return out[:, 0, :]


ROW_CHUNK = 128
COL_CHUNK = 512


def _ffn_kernel(x_ref, gain_ref, scale_ref, shift_ref, gate_ref, wg_ref, wu_ref, wo_ref,
                o_ref, h_scr):
    j = pl.program_id(1)
    tm, d = x_ref.shape

    @pl.when(j == 0)
    def _():
        def body(r, carry):
            rows = pl.ds(pl.multiple_of(r * ROW_CHUNK, ROW_CHUNK), ROW_CHUNK)
            h = _mod_norm(x_ref[rows, :], gain_ref[...], scale_ref[...], shift_ref[...])
            h_scr[rows, :] = h.astype(BF16)
            o_ref[rows, :] = jnp.zeros((ROW_CHUNK, d), F32)
            return carry
        lax.fori_loop(0, tm // ROW_CHUNK, body, 0)

    h = h_scr[...]
    g = _dot(h, wg_ref[...].astype(BF16))
    u = _dot(h, wu_ref[...].astype(BF16))
    a = (_silu(g) * u).astype(BF16)
    for c in range(0, d, COL_CHUNK):
        o_ref[:, c:c + COL_CHUNK] += _dot(a, wo_ref[:, c:c + COL_CHUNK].astype(BF16))

    @pl.when(j == pl.num_programs(1) - 1)
    def _():
        def body(r, carry):
            rows = pl.ds(pl.multiple_of(r * ROW_CHUNK, ROW_CHUNK), ROW_CHUNK)
            o_ref[rows, :] = x_ref[rows, :] + (FFN_RESIDUAL * gate_ref[...]) * o_ref[rows, :]
            return carry
        lax.fori_loop(0, tm // ROW_CHUNK, body, 0)


def _ffn(x, gain, scale, shift, gate, w_in, w_out, lead, tm=1024, tf=256):
    s, d = x.shape
    f = w_out.shape[-2]
    tm = min(tm, s)
    nf = f // tf
    vec = pl.BlockSpec((1, d), lambda i, j: (0, 0))
    return pl.pallas_call(
        _ffn_kernel,
        grid=(s // tm, nf),
        in_specs=[
            pl.BlockSpec((tm, d), lambda i, j: (i, 0)),
            vec, vec, vec, vec,
            _wspec(lead, (d, tf), lambda i, j: (0, j)),
            _wspec(lead, (d, tf), lambda i, j: (0, j + nf)),
            _wspec(lead, (tf, d), lambda i, j: (j, 0)),
        ],
        out_specs=pl.BlockSpec((tm, d), lambda i, j: (i, 0)),
        out_shape=jax.ShapeDtypeStruct((s, d), F32),
        scratch_shapes=[pltpu.VMEM((tm, d), BF16)],
        compiler_params=_cparams(("arbitrary", "arbitrary")),
        name="ffn",
    )(x, gain, scale, shift, gate, w_in, w_in, w_out)


def _head_rms_norm(y, gain):
    outs = []
    for c in range(0, y.shape[1], ATTN_HEAD_DIM):
        yc = y[:, c:c + ATTN_HEAD_DIM]
        outs.append(yc * lax.rsqrt(jnp.mean(yc * yc, axis=-1, keepdims=True) + NORM_EPS))
    return jnp.concatenate(outs, axis=1) * gain


def _normproj_kernel(x_ref, gain_ref, scale_ref, shift_ref, w_ref, hgain_ref, o_ref, h_ref):
    j = pl.program_id(1)
    tm = x_ref.shape[0]

    @pl.when(j == 0)
    def _():
        def body(r, carry):
            rows = pl.ds(pl.multiple_of(r * ROW_CHUNK, ROW_CHUNK), ROW_CHUNK)
            h = _mod_norm(x_ref[rows, :], gain_ref[...], scale_ref[...], shift_ref[...])
            h_ref[rows, :] = h.astype(BF16)
            return carry
        lax.fori_loop(0, tm // ROW_CHUNK, body, 0)

    y = _dot(h_ref[...], w_ref[...].astype(BF16))
    o_ref[...] = _head_rms_norm(y, hgain_ref[...]).astype(o_ref.dtype)


def _norm_qk_proj(x, gain, scale, shift, w, lead, hgain, tm=1024, tn=512):
    s, d = x.shape
    n = 2 * ATTN_WIDTH
    tm = min(tm, s)
    vec = pl.BlockSpec((1, d), lambda i, j: (0, 0))
    return pl.pallas_call(
        _normproj_kernel,
        grid=(s // tm, n // tn),
        in_specs=[
            pl.BlockSpec((tm, d), lambda i, j: (i, 0)),
            vec, vec, vec,
            _wspec(lead, (d, tn), lambda i, j: (0, j)),
            pl.BlockSpec((1, tn), lambda i, j: (0, j)),
        ],
        out_specs=[
            pl.BlockSpec((tm, tn), lambda i, j: (i, j)),
            pl.BlockSpec((tm, d), lambda i, j: (i, 0)),
        ],
        out_shape=[jax.ShapeDtypeStruct((s, n), BF16), jax.ShapeDtypeStruct((s, d), BF16)],
        compiler_params=_cparams(("arbitrary", "arbitrary")),
        name="norm_qk_proj",
    )(x, gain, scale, shift, w, hgain)


def _proj_kernel(h_ref, w_ref, o_ref):
    o_ref[...] = _dot(h_ref[...], w_ref[...].astype(BF16)).astype(o_ref.dtype)


def _proj(h, w, lead, col0, n, out_dtype, tm=1024, tn=512):
    s, d = h.shape
    tm = min(tm, s)
    tn = min(tn, n)
    off = col0 // tn
    return pl.pallas_call(
        _proj_kernel,
        grid=(s // tm, n // tn),
        in_specs=[
            pl.BlockSpec((tm, d), lambda i, j: (i, 0)),
            _wspec(lead, (d, tn), lambda i, j: (0, j + off)),
        ],
        out_specs=pl.BlockSpec((tm, tn), lambda i, j: (i, j)),
        out_shape=jax.ShapeDtypeStruct((s, n), out_dtype),
        compiler_params=_cparams(("arbitrary", "arbitrary")),
        name="proj",
    )(h, w)


def _t5_bucket(dist):
    n = jnp.maximum(dist, 0)
    max_exact = REL_BUCKETS // 2
    ratio = jnp.log(jnp.maximum(n, 1).astype(F32) / max_exact) / math.log(REL_MAX_DIST / max_exact)
    large = max_exact + (ratio * (REL_BUCKETS - max_exact)).astype(jnp.int32)
    large = jnp.minimum(large, REL_BUCKETS - 1)
    return jnp.where(n < max_exact, n, large)


def _moba_kernel(rel_ref, bkt_own_ref, bkt_prev_ref, q_ref, k_ref, v_ref, o_ref,
                 kmean_scr, bias_own_scr, bias_prev_scr, qaug_scr, m_scr, l_scr, acc_scr):
    h = pl.program_id(0)
    i = pl.program_id(1)
    blk = MOBA_BLOCK
    nb = k_ref.shape[0] // blk
    scale = ATTN_HEAD_DIM ** -0.5

    @pl.when(i == 0)
    def _():
        kmean_scr[...] = jnp.zeros(kmean_scr.shape, F32)

        def mean_body(b, carry):
            rows = pl.ds(pl.multiple_of(b * blk, blk), blk)
            kmean_scr[pl.ds(b, 1), :] = jnp.mean(k_ref[rows, :].astype(F32), axis=0, keepdims=True)
            return carry
        lax.fori_loop(0, nb, mean_body, 0)

        bias_own_scr[...] = jnp.zeros((blk, blk), F32)
        bias_prev_scr[...] = jnp.zeros((blk, blk), F32)

        def bias_body(b, carry):
            val = rel_ref[b, h]
            bias_own_scr[...] += jnp.where(bkt_own_ref[...] == b, val, 0.0)
            bias_prev_scr[...] += jnp.where(bkt_prev_ref[...] == b, val, 0.0)
            return carry
        lax.fori_loop(0, REL_BUCKETS, bias_body, 0)

    q = q_ref[...]

    km = kmean_scr[...]
    km_hi = km.astype(BF16)
    km_lo = (km - km_hi.astype(F32)).astype(BF16)
    gate = _dot_nt(q, km_hi) + _dot_nt(q, km_lo)
    col = lax.broadcasted_iota(jnp.int32, gate.shape, 1)
    colf = col.astype(F32)
    gate = jnp.where(col < i, gate, -jnp.inf)
    sel = jnp.zeros(gate.shape, F32)
    for _ in range(min(MOBA_TOPK, nb)):
        top = jnp.max(gate, axis=1, keepdims=True)
        idx = jnp.min(jnp.where(gate == top, colf, float(LANES)), axis=1, keepdims=True)
        hit = colf == idx
        sel = jnp.where(hit & (top > -jnp.inf), 1.0, sel)
        gate = jnp.where(hit, -jnp.inf, gate)
    mask_bias = jnp.where(sel > 0.0, 0.0, NEG_INF).astype(BF16)
    qaug_scr[:, :ATTN_HEAD_DIM] = q
    qaug_scr[:, ATTN_HEAD_DIM:] = mask_bias

    own = pl.ds(pl.multiple_of(i * blk, blk), blk)
    s = _dot_nt(q, k_ref[own, :]) * scale + bias_own_scr[...]
    rr = lax.broadcasted_iota(jnp.int32, (blk, blk), 0)
    cc = lax.broadcasted_iota(jnp.int32, (blk, blk), 1)
    s = jnp.where(cc <= rr, s, NEG_INF)
    m0 = jnp.max(s, axis=1, keepdims=True)
    p = jnp.exp(s - m0)
    m_scr[...] = m0
    l_scr[...] = jnp.sum(p, axis=1, keepdims=True)
    acc_scr[...] = _dot(p.astype(BF16), v_ref[own, :])

    def past_block(j, bias):
        rows = pl.ds(pl.multiple_of(j * blk, blk), blk)
        lane = lax.broadcasted_iota(jnp.int32, (blk, LANES), 1)
        onehot = jnp.where(lane == j, 1.0, 0.0).astype(BF16)
        kaug = jnp.concatenate([k_ref[rows, :], onehot], axis=1)
        s = _dot_nt(qaug_scr[...], kaug) * scale + bias
        m_old = m_scr[...]
        m_new = jnp.maximum(m_old, jnp.max(s, axis=1, keepdims=True))
        alpha = jnp.exp(m_old - m_new)
        p = jnp.exp(s - m_new)
        m_scr[...] = m_new
        l_scr[...] = alpha * l_scr[...] + jnp.sum(p, axis=1, keepdims=True)
        acc_scr[...] = alpha * acc_scr[...] + _dot(p.astype(BF16), v_ref[rows, :])

    @pl.when(i >= 1)
    def _():
        past_block(i - 1, bias_prev_scr[...])

    far_bias = rel_ref[REL_BUCKETS - 1, h]

    def far_body(j, carry):
        past_block(j, far_bias)
        return carry
    lax.fori_loop(0, jnp.maximum(i - 1, 0), far_body, 0)

    o_ref[...] = (acc_scr[...] / l_scr[...]).astype(o_ref.dtype)


def _moba_attention(qk, v, rel_bias):
    s = qk.shape[0]
    blk, hd, nh = MOBA_BLOCK, ATTN_HEAD_DIM, ATTN_HEADS
    assert s % blk == 0 and s // blk <= LANES
    assert blk >= REL_MAX_DIST
    r = jnp.arange(blk)
    dist_own = r[:, None] - r[None, :]
    bkt_own = _t5_bucket(dist_own)
    bkt_prev = _t5_bucket(dist_own + blk)
    const = lambda h, i: (0, 0)
    return pl.pallas_call(
        _moba_kernel,
        grid=(nh, s // blk),
        in_specs=[
            pl.BlockSpec(memory_space=pltpu.SMEM),
            pl.BlockSpec((blk, blk), const),
            pl.BlockSpec((blk, blk), const),
            pl.BlockSpec((blk, hd), lambda h, i: (i, h)),
            pl.BlockSpec((s, hd), lambda h, i: (0, nh + h)),
            pl.BlockSpec((s, hd), lambda h, i: (0, h)),
        ],
        out_specs=pl.BlockSpec((blk, hd), lambda h, i: (i, h)),
        out_shape=jax.ShapeDtypeStruct((s, nh * hd), BF16),
        scratch_shapes=[
            pltpu.VMEM((LANES, hd), F32),
            pltpu.VMEM((blk, blk), F32),
            pltpu.VMEM((blk, blk), F32),
            pltpu.VMEM((blk, hd + LANES), BF16),
            pltpu.VMEM((blk, 1), F32),
            pltpu.VMEM((blk, 1), F32),
            pltpu.VMEM((blk, hd), F32),
        ],
        compiler_params=_cparams(("arbitrary", "arbitrary")),
        name="moba_attention",
    )(rel_bias, bkt_own, bkt_prev, qk, qk, v)


SSD_TAIL = 8
SSD_CONV_COLS = 512


def _softplus(x):
    return jnp.maximum(x, 0.0) + jnp.log(1.0 + jnp.exp(-jnp.abs(x)))


def _split3(x):
    hi = x.astype(BF16)
    r1 = x - hi.astype(F32)
    mid = r1.astype(BF16)
    lo = (r1 - mid.astype(F32)).astype(BF16)
    return hi, mid, lo


def _ssd_kernel(xbc_ref, z_ref, dt_ref, dtt_ref, wconv_ref, bconv_ref, dtb_ref, alog_ref,
                dtbt_ref, alogt_ref, dskip_ref, norm_ref, y_ref,
                ext_scr, xact_scr, state_scr, y_scr):
    c = pl.program_id(0)
    L, P, N, W = SSD_CHUNK, SSD_HEAD_DIM, SSD_STATE, SSD_WIDTH

    @pl.when(c == 0)
    def _():
        ext_scr[0:SSD_TAIL, :] = jnp.zeros((SSD_TAIL, SSD_XBC), F32)
        state_scr[...] = jnp.zeros(state_scr.shape, F32)

    for c0 in range(0, SSD_XBC, SSD_CONV_COLS):
        cols = slice(c0, c0 + SSD_CONV_COLS)
        ext_scr[SSD_TAIL:SSD_TAIL + L, cols] = xbc_ref[:, cols].astype(F32)
        acc = jnp.broadcast_to(bconv_ref[:, cols], (L, SSD_CONV_COLS))
        for j in range(SSD_CONV):
            start = SSD_TAIL - (SSD_CONV - 1) + j
            acc = acc + wconv_ref[j:j + 1, cols] * ext_scr[start:start + L, cols]
        xact_scr[:, cols] = _silu(acc)
        ext_scr[0:SSD_TAIL, cols] = ext_scr[L:L + SSD_TAIL, cols]

    dtv = _softplus(dt_ref[...] + dtb_ref[...])
    ad = dtv * (-jnp.exp(alog_ref[...]))
    dtt = _softplus(dtt_ref[...] + dtbt_ref[...])
    adt = dtt * (-jnp.exp(alogt_ref[...]))
    rr = lax.broadcasted_iota(jnp.int32, (L, L), 0)
    cc = lax.broadcasted_iota(jnp.int32, (L, L), 1)
    causal = rr >= cc
    lower = jnp.where(causal, 1.0, 0.0).astype(BF16)
    upper = jnp.where(rr <= cc, 1.0, 0.0).astype(BF16)
    a_cs = sum(_dot(lower, t) for t in _split3(ad))
    a_cst = sum(_dot(t, upper) for t in _split3(adt))

    heads_per_group = SSD_HEADS // SSD_GROUPS
    for g in range(SSD_GROUPS):
        bg = xact_scr[:, W + g * N:W + (g + 1) * N].astype(BF16)
        cg = xact_scr[:, W + SSD_BC + g * N:W + SSD_BC + (g + 1) * N].astype(BF16)
        cb = _dot_nt(cg, bg)
        for r in range(heads_per_group):
            h = g * heads_per_group + r
            hc = slice(h * P, (h + 1) * P)
            col = a_cs[:, h:h + 1]
            row = a_cst[h:h + 1, :]
            dec = jnp.exp(jnp.where(causal, col - row, -jnp.inf))
            m = (cb * dec).astype(BF16)
            xs = xact_scr[:, hc]
            xd = xs * dtv[:, h:h + 1]
            st = state_scr[hc, :]
            y = _dot(m, xd.astype(BF16)) + _dot_nt(cg, st.astype(BF16)) * jnp.exp(col)
            last = a_cs[L - 1:L, h:h + 1]
            xe = (xd * jnp.exp(last - col)).astype(BF16)
            new = lax.dot_general(xe, bg, (((0,), (0,)), ((), ())), preferred_element_type=F32)
            state_scr[hc, :] = jnp.exp(last) * st + new
            y_scr[:, hc] = y + xs * dskip_ref[:, hc]

    group_w = W // SSD_GROUPS
    for g in range(SSD_GROUPS):
        cols = slice(g * group_w, (g + 1) * group_w)
        yg = y_scr[:, cols] * _silu(z_ref[:, cols].astype(F32))
        yg = yg * lax.rsqrt(jnp.mean(yg * yg, axis=-1, keepdims=True) + NORM_EPS)
        y_ref[:, cols] = (yg * norm_ref[:, cols]).astype(y_ref.dtype)


def _ssd(xbc, z, dt, w_conv, b_conv, dt_bias, a_log, d_skip, norm):
    s = xbc.shape[0]
    L, nh = SSD_CHUNK, SSD_HEADS
    assert s % L == 0
    dtt = dt[:, :nh].T
    pad = lambda v: jnp.pad(v, (0, LANES - nh)).reshape(1, LANES)
    full = lambda r, cdim: pl.BlockSpec((r, cdim), lambda c: (0, 0))
    return pl.pallas_call(
        _ssd_kernel,
        grid=(s // L,),
        in_specs=[
            pl.BlockSpec((L, SSD_XBC), lambda c: (c, 0)),
            pl.BlockSpec((L, SSD_WIDTH), lambda c: (c, 0)),
            pl.BlockSpec((L, LANES), lambda c: (c, 0)),
            pl.BlockSpec((nh, L), lambda c: (0, c)),
            full(SSD_CONV, SSD_XBC), full(1, SSD_XBC),
            full(1, LANES), full(1, LANES), full(nh, 1), full(nh, 1),
            full(1, SSD_WIDTH), full(1, SSD_WIDTH),
        ],
        out_specs=pl.BlockSpec((L, SSD_WIDTH), lambda c: (c, 0)),
        out_shape=jax.ShapeDtypeStruct((s, SSD_WIDTH), BF16),
        scratch_shapes=[
            pltpu.VMEM((SSD_TAIL + L, SSD_XBC), F32),
            pltpu.VMEM((L, SSD_XBC), F32),
            pltpu.VMEM((SSD_WIDTH, SSD_STATE), F32),
            pltpu.VMEM((L, SSD_WIDTH), F32),
        ],
        compiler_params=_cparams(("arbitrary",)),
        name="ssd_scan",
    )(xbc, z, dt, dtt, w_conv, b_conv.reshape(1, SSD_XBC), pad(dt_bias), pad(a_log),
      dt_bias.reshape(nh, 1), a_log.reshape(nh, 1),
      jnp.repeat(d_skip, SSD_HEAD_DIM).reshape(1, SSD_WIDTH), norm.reshape(1, SSD_WIDTH))


HALO = 16


def _merge_kernel(ya_ref, ys_ref, cb_ref, cc_ref, cx_ref, hc_ref, hx_ref, ga_ref, gs_ref, gc_ref,
                  wsc_ref, wa_ref, ws_ref, wc_ref, o_ref, ext_scr, yc_scr):
    i = pl.program_id(0)
    j = pl.program_id(1)
    tm = ya_ref.shape[0]

    @pl.when(j == 0)
    def _():
        halo = hc_ref[...].astype(F32) * hx_ref[...].astype(F32)
        ext_scr[0:HALO, :] = jnp.where(i > 0, halo, 0.0)
        for r0 in range(0, tm, ROW_CHUNK):
            rows = slice(r0, r0 + ROW_CHUNK)
            ext_scr[HALO + r0:HALO + r0 + ROW_CHUNK, :] = cc_ref[rows, :].astype(F32) * cx_ref[rows, :].astype(F32)
        for r0 in range(0, tm, ROW_CHUNK):
            rows = slice(r0, r0 + ROW_CHUNK)
            acc = jnp.zeros((ROW_CHUNK, CONV_WIDTH), F32)
            for k in range(CONV_K):
                start = HALO + r0 - (CONV_K - 1) + k
                acc = acc + wsc_ref[k:k + 1, :] * ext_scr[start:start + ROW_CHUNK, :]
            yc_scr[rows, :] = (cb_ref[rows, :].astype(F32) * acc).astype(BF16)

    merged = (_sigmoid(ga_ref[...].astype(F32)) * _dot(ya_ref[...], wa_ref[...].astype(BF16))
              + _sigmoid(gs_ref[...].astype(F32)) * _dot(ys_ref[...], ws_ref[...].astype(BF16))
              + _sigmoid(gc_ref[...].astype(F32)) * _dot(yc_scr[...], wc_ref[...].astype(BF16)))
    o_ref[...] = merged.astype(o_ref.dtype)


def _branch_merge(y_attn, y_ssd, tail, w_sc, w_a, w_s, w_c, lead, tm=1024, tn=256):
    s = y_attn.shape[0]
    d = w_a.shape[-1]
    tm = min(tm, s)
    cw = CONV_WIDTH
    g0 = 3 * cw // tn
    gd = d // tn
    row = lambda width, cb: pl.BlockSpec((tm, width), lambda i, j: (i, cb))
    halo = lambda cb: pl.BlockSpec((HALO, cw), lambda i, j: (jnp.maximum(i * (tm // HALO) - 1, 0), cb))
    gate = lambda k: pl.BlockSpec((tm, tn), lambda i, j: (i, g0 + k * gd + j))
    wcol = lambda kdim: _wspec(lead, (kdim, tn), lambda i, j: (0, j))
    return pl.pallas_call(
        _merge_kernel,
        grid=(s // tm, d // tn),
        in_specs=[
            row(ATTN_WIDTH, 0), row(SSD_WIDTH, 0),
            row(cw, 0), row(cw, 1), row(cw, 2), halo(1), halo(2),
            gate(0), gate(1), gate(2),
            pl.BlockSpec((CONV_K, cw), lambda i, j: (0, 0)),
            wcol(ATTN_WIDTH), wcol(SSD_WIDTH), wcol(cw),
        ],
        out_specs=pl.BlockSpec((tm, tn), lambda i, j: (i, j)),
        out_shape=jax.ShapeDtypeStruct((s, d), BF16),
        scratch_shapes=[pltpu.VMEM((HALO + tm, cw), F32), pltpu.VMEM((tm, cw), BF16)],
        compiler_params=_cparams(("arbitrary", "arbitrary")),
        name="branch_merge",
    )(y_attn, y_ssd, tail, tail, tail, tail, tail, tail, tail, tail, w_sc, w_a, w_s, w_c)


def _outproj_kernel(m_ref, w_ref, x_ref, gate_ref, o_ref):
    o_ref[...] = x_ref[...] + gate_ref[...] * _dot(m_ref[...], w_ref[...].astype(BF16))


def _out_proj(merged, w, lead, x, gate, tm=1024, tn=512):
    s, d = x.shape
    tm = min(tm, s)
    return pl.pallas_call(
        _outproj_kernel,
        grid=(s // tm, d // tn),
        in_specs=[
            pl.BlockSpec((tm, merged.shape[1]), lambda i, j: (i, 0)),
            _wspec(lead, (merged.shape[1], tn), lambda i, j: (0, j)),
            pl.BlockSpec((tm, tn), lambda i, j: (i, j)),
            pl.BlockSpec((1, tn), lambda i, j: (0, j)),
        ],
        out_specs=pl.BlockSpec((tm, tn), lambda i, j: (i, j)),
        out_shape=jax.ShapeDtypeStruct((s, d), F32),
        compiler_params=_cparams(("arbitrary", "arbitrary")),
        name="out_proj",
    )(merged, w, x, gate)


def _token_mix(x, gain, scale, shift, gate, l, w_mix_in, qk_norm, rel_bias, w_ssd_conv, b_ssd_conv,
               ssd_dt_bias, ssd_a_log, ssd_d, ssd_norm, w_sc_conv, w_br_attn, w_br_ssd, w_br_conv,
               w_mix_out):
    d = x.shape[1]
    lead = (l,)
    c_v = 2 * ATTN_WIDTH
    c_z = c_v + ATTN_WIDTH
    c_xbc = c_z + SSD_WIDTH
    c_dt = c_xbc + SSD_XBC
    c_tail = c_dt + SSD_HEADS
    n_tail = 3 * CONV_WIDTH + 3 * d
    assert w_mix_in.shape[-1] == c_tail + n_tail

    hgain = jnp.concatenate([jnp.tile(qk_norm[l, 0], ATTN_HEADS), jnp.tile(qk_norm[l, 1], ATTN_HEADS)])[None]
    qk, h = _norm_qk_proj(x, gain, scale, shift, w_mix_in, lead, hgain)
    v = _proj(h, w_mix_in, lead, c_v, ATTN_WIDTH, BF16)
    z = _proj(h, w_mix_in, lead, c_z, SSD_WIDTH, BF16)
    xbc = _proj(h, w_mix_in, lead, c_xbc, SSD_XBC, BF16)
    w_dt = jnp.pad(w_mix_in[l, :, c_dt:c_tail], ((0, 0), (0, LANES - SSD_HEADS)))
    dt = _proj(h, w_dt, (), 0, LANES, F32)
    tail = _proj(h, w_mix_in[l, :, c_tail:], (), 0, n_tail, BF16)

    y_attn = _moba_attention(qk, v, rel_bias)
    y_ssd = _ssd(xbc, z, dt, w_ssd_conv[l], b_ssd_conv[l], ssd_dt_bias[l], ssd_a_log[l], ssd_d[l], ssd_norm[l])
    merged = _branch_merge(y_attn, y_ssd, tail, w_sc_conv[l], w_br_attn, w_br_ssd, w_br_conv, lead)
    return _out_proj(merged, w_mix_out, lead, x, gate)


def kernel(x, c, w_ada, b_ada, norm_gain, w_ffn_in, w_ffn_out, w_mix_in, qk_norm, rel_bias, w_ssd_conv, b_ssd_conv, ssd_dt_bias, ssd_a_log, ssd_d, ssd_norm, w_sc_conv, w_br_attn, w_br_ssd, w_br_conv, w_mix_out):
    b, s, d = x.shape
    depth = w_ada.shape[0]
    assert b == 1 and s % math.lcm(MOBA_BLOCK, SSD_CHUNK) == 0
    xs = x.reshape(s, d)
    ada = _ada_proj(c, w_ada, b_ada).reshape(depth, N_SUBLAYERS, 3, 1, d)
    for l in range(depth):
        mod = lambda i: (norm_gain[l, i][None], ada[l, i, 1], ada[l, i, 0], ada[l, i, 2])
        xs = _ffn(xs, *mod(0), w_ffn_in, w_ffn_out, (l, 0))
        xs = _token_mix(xs, *mod(1), l, w_mix_in, qk_norm, rel_bias, w_ssd_conv, b_ssd_conv,
                        ssd_dt_bias, ssd_a_log, ssd_d, ssd_norm, w_sc_conv,
                        w_br_attn, w_br_ssd, w_br_conv, w_mix_out)
        xs = _ffn(xs, *mod(2), w_ffn_in, w_ffn_out, (l, 1))
    return xs.reshape(b, s, d)
```

```python
import functools
import math

import jax
import jax.numpy as jnp
from jax import lax
from jax.experimental import pallas as pl
from jax.experimental.pallas import tpu as pltpu

F32 = jnp.float32
BF16 = jnp.bfloat16

ATTN_HEADS = 8
ATTN_HEAD_DIM = 128
ATTN_WIDTH = ATTN_HEADS * ATTN_HEAD_DIM
MOBA_BLOCK = 256
MOBA_TOPK = 3
REL_BUCKETS = 32
REL_MAX_DIST = 128
SSD_HEADS = 32
SSD_HEAD_DIM = 64
SSD_WIDTH = SSD_HEADS * SSD_HEAD_DIM
SSD_GROUPS = 4
SSD_STATE = 128
SSD_CONV = 4
SSD_CHUNK = 256
SSD_BC = SSD_GROUPS * SSD_STATE
SSD_XBC = SSD_WIDTH + 2 * SSD_BC
CONV_WIDTH = 1024
CONV_K = 3
N_SUBLAYERS = 3
FFN_RESIDUAL = 0.5
NORM_EPS = 1e-6
NEG_INF = -1e30

LANES = 128
VMEM_LIMIT_BYTES = 58 * 1024 * 1024


def _cparams(semantics):
    return pltpu.CompilerParams(dimension_semantics=semantics, vmem_limit_bytes=VMEM_LIMIT_BYTES)


def _sigmoid(x):
    return 1.0 / (1.0 + jnp.exp(-x))


def _silu(x):
    return x * _sigmoid(x)


def _dot(a, b):
    return jnp.dot(a, b, preferred_element_type=F32)


def _dot_nt(a, b):
    return lax.dot_general(a, b, (((1,), (1,)), ((), ())), preferred_element_type=F32)


def _wspec(lead, block, index_map):
    return pl.BlockSpec((None,) * len(lead) + block, lambda i, j: lead + index_map(i, j))


def _mod_norm(x, gain, scale, shift):
    y = x * lax.rsqrt(jnp.mean(x * x, axis=-1, keepdims=True) + NORM_EPS)
    return (y * gain) * (1.0 + scale) + shift


def _ada_kernel(c_ref, w_ref, b_ref, o_ref):
    cond = _silu(c_ref[...]).astype(BF16)
    o_ref[...] = _dot(cond, w_ref[...].astype(BF16)) + b_ref[...]


def _ada_proj(c, w_ada, b_ada, tn=1024):
    depth, d, n = w_ada.shape
    c8 = jnp.broadcast_to(c, (8, d))
    out = pl.pallas_call(
        _ada_kernel,
        grid=(depth, n // tn),
        in_specs=[
            pl.BlockSpec((8, d), lambda l, j: (0, 0)),
            pl.BlockSpec((None, d, tn), lambda l, j: (l, 0, j)),
            pl.BlockSpec((None, 1, tn), lambda l, j: (l, 0, j)),
        ],
        out_specs=pl.BlockSpec((None, 8, tn), lambda l, j: (l, 0, j)),
        out_shape=jax.ShapeDtypeStruct((depth, 8, n), F32),
        compiler_params=_cparams(("arbitrary", "arbitrary")),
        name="ada_proj",
    )(c8, w_ada, b_ada.reshape(depth, 1, n))
    return out[:, 0, :]


ROW_CHUNK = 128
COL_CHUNK = 512


def _ffn_kernel(x_ref, gain_ref, scale_ref, shift_ref, gate_ref, wg_ref, wu_ref, wo_ref,
                o_ref, h_scr):
    j = pl.program_id(1)
    tm, d = x_ref.shape

    @pl.when(j == 0)
    def _():
        def body(r, carry):
            rows = pl.ds(pl.multiple_of(r * ROW_CHUNK, ROW_CHUNK), ROW_CHUNK)
            h = _mod_norm(x_ref[rows, :], gain_ref[...], scale_ref[...], shift_ref[...])
            h_scr[rows, :] = h.astype(BF16)
            o_ref[rows, :] = jnp.zeros((ROW_CHUNK, d), F32)
            return carry
        lax.fori_loop(0, tm // ROW_CHUNK, body, 0)

    h = h_scr[...]
    g = _dot(h, wg_ref[...].astype(BF16))
    u = _dot(h, wu_ref[...].astype(BF16))
    a = (_silu(g) * u).astype(BF16)
    for c in range(0, d, COL_CHUNK):
        o_ref[:, c:c + COL_CHUNK] += _dot(a, wo_ref[:, c:c + COL_CHUNK].astype(BF16))

    @pl.when(j == pl.num_programs(1) - 1)
    def _():
        def body(r, carry):
            rows = pl.ds(pl.multiple_of(r * ROW_CHUNK, ROW_CHUNK), ROW_CHUNK)
            o_ref[rows, :] = x_ref[rows, :] + (FFN_RESIDUAL * gate_ref[...]) * o_ref[rows, :]
            return carry
        lax.fori_loop(0, tm // ROW_CHUNK, body, 0)


def _ffn(x, gain, scale, shift, gate, w_in, w_out, lead, tm=1024, tf=256):
    s, d = x.shape
    f = w_out.shape[-2]
    tm = min(tm, s)
    nf = f // tf
    vec = pl.BlockSpec((1, d), lambda i, j: (0, 0))
    return pl.pallas_call(
        _ffn_kernel,
        grid=(s // tm, nf),
        in_specs=[
            pl.BlockSpec((tm, d), lambda i, j: (i, 0)),
            vec, vec, vec, vec,
            _wspec(lead, (d, tf), lambda i, j: (0, j)),
            _wspec(lead, (d, tf), lambda i, j: (0, j + nf)),
            _wspec(lead, (tf, d), lambda i, j: (j, 0)),
        ],
        out_specs=pl.BlockSpec((tm, d), lambda i, j: (i, 0)),
        out_shape=jax.ShapeDtypeStruct((s, d), F32),
        scratch_shapes=[pltpu.VMEM((tm, d), BF16)],
        compiler_params=_cparams(("arbitrary", "arbitrary")),
        name="ffn",
    )(x, gain, scale, shift, gate, w_in, w_in, w_out)


def _head_rms_norm(y, gain):
    outs = []
    for c in range(0, y.shape[1], ATTN_HEAD_DIM):
        yc = y[:, c:c + ATTN_HEAD_DIM]
        outs.append(yc * lax.rsqrt(jnp.mean(yc * yc, axis=-1, keepdims=True) + NORM_EPS))
    return jnp.concatenate(outs, axis=1) * gain


def _normproj_kernel(x_ref, gain_ref, scale_ref, shift_ref, w_ref, hgain_ref, o_ref, h_ref):
    j = pl.program_id(1)
    tm = x_ref.shape[0]

    @pl.when(j == 0)
    def _():
        def body(r, carry):
            rows = pl.ds(pl.multiple_of(r * ROW_CHUNK, ROW_CHUNK), ROW_CHUNK)
            h = _mod_norm(x_ref[rows, :], gain_ref[...], scale_ref[...], shift_ref[...])
            h_ref[rows, :] = h.astype(BF16)
            return carry
        lax.fori_loop(0, tm // ROW_CHUNK, body, 0)

    y = _dot(h_ref[...], w_ref[...].astype(BF16))
    o_ref[...] = _head_rms_norm(y, hgain_ref[...]).astype(o_ref.dtype)


def _norm_qk_proj(x, gain, scale, shift, w, lead, hgain, tm=1024, tn=512):
    s, d = x.shape
    n = 2 * ATTN_WIDTH
    tm = min(tm, s)
    vec = pl.BlockSpec((1, d), lambda i, j: (0, 0))
    return pl.pallas_call(
        _normproj_kernel,
        grid=(s // tm, n // tn),
        in_specs=[
            pl.BlockSpec((tm, d), lambda i, j: (i, 0)),
            vec, vec, vec,
            _wspec(lead, (d, tn), lambda i, j: (0, j)),
            pl.BlockSpec((1, tn), lambda i, j: (0, j)),
        ],
        out_specs=[
            pl.BlockSpec((tm, tn), lambda i, j: (i, j)),
            pl.BlockSpec((tm, d), lambda i, j: (i, 0)),
        ],
        out_shape=[jax.ShapeDtypeStruct((s, n), BF16), jax.ShapeDtypeStruct((s, d), BF16)],
        compiler_params=_cparams(("arbitrary", "arbitrary")),
        name="norm_qk_proj",
    )(x, gain, scale, shift, w, hgain)


def _proj_kernel(h_ref, w_ref, o_ref):
    o_ref[...] = _dot(h_ref[...], w_ref[...].astype(BF16)).astype(o_ref.dtype)


def _proj(h, w, lead, col0, n, out_dtype, tm=1024, tn=512):
    s, d = h.shape
    tm = min(tm, s)
    tn = min(tn, n)
    off = col0 // tn
    return pl.pallas_call(
        _proj_kernel,
        grid=(s // tm, n // tn),
        in_specs=[
            pl.BlockSpec((tm, d), lambda i, j: (i, 0)),
            _wspec(lead, (d, tn), lambda i, j: (0, j + off)),
        ],
        out_specs=pl.BlockSpec((tm, tn), lambda i, j: (i, j)),
        out_shape=jax.ShapeDtypeStruct((s, n), out_dtype),
        compiler_params=_cparams(("arbitrary", "arbitrary")),
        name="proj",
    )(h, w)


def _t5_bucket(dist):
    n = jnp.maximum(dist, 0)
    max_exact = REL_BUCKETS // 2
    ratio = jnp.log(jnp.maximum(n, 1).astype(F32) / max_exact) / math.log(REL_MAX_DIST / max_exact)
    large = max_exact + (ratio * (REL_BUCKETS - max_exact)).astype(jnp.int32)
    large = jnp.minimum(large, REL_BUCKETS - 1)
    return jnp.where(n < max_exact, n, large)


ATTN_VIS_STEPS = (8, 16, 24, 32)
ATTN_QK_AHEAD = 5
ATTN_SUM_ROWS = 16
LOG2E = 1.4426950408889634
ATTN_Q_PRESCALE = ATTN_HEAD_DIM ** -0.5 * LOG2E


def _moba_kernel(rel_ref, bkt_own_ref, bkt_prev_ref, q_ref, k_ref, v_ref, o_ref,
                 kmean_scr, bias_own_scr, bias_prev_scr, vt_scr, mask_scr):
    h = pl.program_id(0)
    i = pl.program_id(1)
    blk, hd = MOBA_BLOCK, ATTN_HEAD_DIM
    nb = k_ref.shape[0] // blk

    @pl.when(i == 0)
    def _():
        kmean_scr[...] = jnp.zeros(kmean_scr.shape, F32)

        ones_row = jnp.where(lax.broadcasted_iota(jnp.int32, (ATTN_SUM_ROWS, blk), 0) == 0, 1.0, 0.0)

        def block_body(b, carry):
            rows = pl.ds(pl.multiple_of(b * blk, blk), blk)
            kmean_scr[pl.ds(b, 1), :] = jnp.mean(k_ref[rows, :].astype(F32), axis=0, keepdims=True)
            vt_scr[b, :hd, :] = v_ref[rows, :].astype(F32).T.astype(BF16)
            vt_scr[b, hd:, :] = ones_row.astype(BF16)
            return carry
        lax.fori_loop(0, nb, block_body, 0)

        key = lax.broadcasted_iota(jnp.int32, (blk, blk), 0)
        qry = lax.broadcasted_iota(jnp.int32, (blk, blk), 1)
        bias_own_scr[...] = jnp.where(key <= qry, 0.0, NEG_INF)
        bias_prev_scr[...] = jnp.zeros((blk, blk), F32)

        def bias_body(b, carry):
            val = rel_ref[b, h] * LOG2E
            bias_own_scr[...] += jnp.where(bkt_own_ref[...] == b, val, 0.0)
            bias_prev_scr[...] += jnp.where(bkt_prev_ref[...] == b, val, 0.0)
            return carry
        lax.fori_loop(0, REL_BUCKETS, bias_body, 0)

    q = q_ref[...]

    km = kmean_scr[...]
    km_hi = km.astype(BF16)
    km_lo = (km - km_hi.astype(F32)).astype(BF16)
    gate = _dot_nt(km_hi, q) + _dot_nt(km_lo, q)
    rowb = lax.broadcasted_iota(jnp.int32, gate.shape, 0)
    rowf = rowb.astype(F32)
    gate = jnp.where(rowb < i, gate, -jnp.inf)
    sel = jnp.zeros(gate.shape, F32)
    for _ in range(min(MOBA_TOPK, nb)):
        top = jnp.max(gate, axis=0, keepdims=True)
        idx = jnp.min(jnp.where(gate == top, rowf, float(LANES)), axis=0, keepdims=True)
        hit = rowf == idx
        sel = jnp.where(hit & (top > -jnp.inf), 1.0, sel)
        gate = jnp.where(hit, -jnp.inf, gate)
    mask_scr[...] = jnp.where(sel > 0.0, 0.0, NEG_INF)

    far_bias = rel_ref[REL_BUCKETS - 1, h] * LOG2E
    visible = i + 1
    lo = 0
    for n_vis in sorted({min(v, nb) for v in ATTN_VIS_STEPS} | {nb}):
        @pl.when((visible > lo) & (visible <= n_vis))
        def _(n_vis=n_vis):
            _moba_tile(n_vis, i, far_bias, q, k_ref, o_ref, bias_own_scr, bias_prev_scr, vt_scr, mask_scr)
        lo = n_vis


def _moba_tile(n_vis, i, far_bias, q, k_ref, o_ref, bias_own_scr, bias_prev_scr, vt_scr, mask_scr):
    blk, hd = MOBA_BLOCK, ATTN_HEAD_DIM
    qt = q.astype(F32).T.astype(BF16)

    def rows(j):
        return pl.ds(pl.multiple_of(j * blk, blk), blk)

    n_far = n_vis - 2
    jp = jnp.maximum(i - 1, 0)
    s_own = _dot(k_ref[rows(i), :], qt)
    s_prev = _dot(k_ref[rows(jp), :], qt) if n_vis > 1 else None
    far_s = {j: _dot(k_ref[j * blk:(j + 1) * blk, :], qt) for j in range(min(ATTN_QK_AHEAD, n_far))}

    t = s_own + bias_own_scr[...]
    m = jnp.max(t, axis=0, keepdims=True)
    acc = _dot(vt_scr[i], jnp.exp2(t - m).astype(BF16))

    if n_vis > 1:
        chosen = jnp.where(i >= 1, mask_scr[pl.ds(jp, 1), :], NEG_INF)
        t = s_prev + bias_prev_scr[...] + chosen
        m_new = jnp.maximum(m, jnp.max(t, axis=0, keepdims=True))
        acc = jnp.exp2(m - m_new) * acc + _dot(vt_scr[jp], jnp.exp2(t - m_new).astype(BF16))
        m = m_new

    for j in range(n_far):
        shift = far_bias + jnp.where(j < i - 1, mask_scr[j:j + 1, :], NEG_INF)
        s = far_s.pop(j)
        ja = j + ATTN_QK_AHEAD
        if ja < n_far:
            far_s[ja] = _dot(k_ref[ja * blk:(ja + 1) * blk, :], qt)
        m_new = jnp.maximum(m, jnp.max(s, axis=0, keepdims=True) + shift)
        p = jnp.exp2(s + (shift - m_new))
        acc = jnp.exp2(m - m_new) * acc + _dot(vt_scr[j], p.astype(BF16))
        m = m_new

    o_ref[...] = (acc[:hd, :] / acc[hd:hd + 1, :]).T.astype(o_ref.dtype)


def _moba_attention(qk, v, rel_bias):
    s = qk.shape[0]
    blk, hd, nh = MOBA_BLOCK, ATTN_HEAD_DIM, ATTN_HEADS
    assert s % blk == 0 and s // blk < LANES
    assert blk >= REL_MAX_DIST
    r = jnp.arange(blk)
    dist_own = r[None, :] - r[:, None]
    bkt_own = _t5_bucket(dist_own)
    bkt_prev = _t5_bucket(dist_own + blk)
    const = lambda h, i: (0, 0)
    return pl.pallas_call(
        _moba_kernel,
        grid=(nh, s // blk),
        in_specs=[
            pl.BlockSpec(memory_space=pltpu.SMEM),
            pl.BlockSpec((blk, blk), const),
            pl.BlockSpec((blk, blk), const),
            pl.BlockSpec((blk, hd), lambda h, i: (i, h)),
            pl.BlockSpec((s, hd), lambda h, i: (0, nh + h)),
            pl.BlockSpec((s, hd), lambda h, i: (0, h)),
        ],
        out_specs=pl.BlockSpec((blk, hd), lambda h, i: (i, h)),
        out_shape=jax.ShapeDtypeStruct((s, nh * hd), BF16),
        scratch_shapes=[
            pltpu.VMEM((LANES, hd), F32),
            pltpu.VMEM((blk, blk), F32),
            pltpu.VMEM((blk, blk), F32),
            pltpu.VMEM((s // blk, hd + ATTN_SUM_ROWS, blk), BF16),
            pltpu.VMEM((LANES, blk), F32),
        ],
        compiler_params=_cparams(("arbitrary", "arbitrary")),
        name="moba_attention",
    )(rel_bias, bkt_own, bkt_prev, qk, qk, v)


SSD_TAIL = 8
SSD_CONV_COLS = 512


def _softplus(x):
    return jnp.maximum(x, 0.0) + jnp.log(1.0 + jnp.exp(-jnp.abs(x)))


def _split3(x):
    hi = x.astype(BF16)
    r1 = x - hi.astype(F32)
    mid = r1.astype(BF16)
    lo = (r1 - mid.astype(F32)).astype(BF16)
    return hi, mid, lo


def _ssd_kernel(xbc_ref, z_ref, dt_ref, dtt_ref, wconv_ref, bconv_ref, dtb_ref, alog_ref,
                dtbt_ref, alogt_ref, dskip_ref, norm_ref, y_ref,
                ext_scr, xact_scr, state_scr, y_scr):
    c = pl.program_id(0)
    L, P, N, W = SSD_CHUNK, SSD_HEAD_DIM, SSD_STATE, SSD_WIDTH

    @pl.when(c == 0)
    def _():
        ext_scr[0:SSD_TAIL, :] = jnp.zeros((SSD_TAIL, SSD_XBC), F32)
        state_scr[...] = jnp.zeros(state_scr.shape, F32)

    for c0 in range(0, SSD_XBC, SSD_CONV_COLS):
        cols = slice(c0, c0 + SSD_CONV_COLS)
        ext_scr[SSD_TAIL:SSD_TAIL + L, cols] = xbc_ref[:, cols].astype(F32)
        acc = jnp.broadcast_to(bconv_ref[:, cols], (L, SSD_CONV_COLS))
        for j in range(SSD_CONV):
            start = SSD_TAIL - (SSD_CONV - 1) + j
            acc = acc + wconv_ref[j:j + 1, cols] * ext_scr[start:start + L, cols]
        xact_scr[:, cols] = _silu(acc)
        ext_scr[0:SSD_TAIL, cols] = ext_scr[L:L + SSD_TAIL, cols]

    dtv = _softplus(dt_ref[...] + dtb_ref[...])
    ad = dtv * (-jnp.exp(alog_ref[...]))
    dtt = _softplus(dtt_ref[...] + dtbt_ref[...])
    adt = dtt * (-jnp.exp(alogt_ref[...]))
    rr = lax.broadcasted_iota(jnp.int32, (L, L), 0)
    cc = lax.broadcasted_iota(jnp.int32, (L, L), 1)
    causal = rr >= cc
    lower = jnp.where(causal, 1.0, 0.0).astype(BF16)
    upper = jnp.where(rr <= cc, 1.0, 0.0).astype(BF16)
    a_cs = sum(_dot(lower, t) for t in _split3(ad))
    a_cst = sum(_dot(t, upper) for t in _split3(adt))

    heads_per_group = SSD_HEADS // SSD_GROUPS
    for g in range(SSD_GROUPS):
        bg = xact_scr[:, W + g * N:W + (g + 1) * N].astype(BF16)
        cg = xact_scr[:, W + SSD_BC + g * N:W + SSD_BC + (g + 1) * N].astype(BF16)
        cb = _dot_nt(cg, bg)
        for r in range(heads_per_group):
            h = g * heads_per_group + r
            hc = slice(h * P, (h + 1) * P)
            col = a_cs[:, h:h + 1]
            row = a_cst[h:h + 1, :]
            dec = jnp.exp(jnp.where(causal, col - row, -jnp.inf))
            m = (cb * dec).astype(BF16)
            xs = xact_scr[:, hc]
            xd = xs * dtv[:, h:h + 1]
            st = state_scr[hc, :]
            y = _dot(m, xd.astype(BF16)) + _dot_nt(cg, st.astype(BF16)) * jnp.exp(col)
            last = a_cs[L - 1:L, h:h + 1]
            xe = (xd * jnp.exp(last - col)).astype(BF16)
            new = lax.dot_general(xe, bg, (((0,), (0,)), ((), ())), preferred_element_type=F32)
            state_scr[hc, :] = jnp.exp(last) * st + new
            y_scr[:, hc] = y + xs * dskip_ref[:, hc]

    group_w = W // SSD_GROUPS
    for g in range(SSD_GROUPS):
        cols = slice(g * group_w, (g + 1) * group_w)
        yg = y_scr[:, cols] * _silu(z_ref[:, cols].astype(F32))
        yg = yg * lax.rsqrt(jnp.mean(yg * yg, axis=-1, keepdims=True) + NORM_EPS)
        y_ref[:, cols] = (yg * norm_ref[:, cols]).astype(y_ref.dtype)


def _ssd(xbc, z, dt, w_conv, b_conv, dt_bias, a_log, d_skip, norm):
    s = xbc.shape[0]
    L, nh = SSD_CHUNK, SSD_HEADS
    assert s % L == 0
    dtt = dt[:, :nh].T
    pad = lambda v: jnp.pad(v, (0, LANES - nh)).reshape(1, LANES)
    full = lambda r, cdim: pl.BlockSpec((r, cdim), lambda c: (0, 0))
    return pl.pallas_call(
        _ssd_kernel,
        grid=(s // L,),
        in_specs=[
            pl.BlockSpec((L, SSD_XBC), lambda c: (c, 0)),
            pl.BlockSpec((L, SSD_WIDTH), lambda c: (c, 0)),
            pl.BlockSpec((L, LANES), lambda c: (c, 0)),
            pl.BlockSpec((nh, L), lambda c: (0, c)),
            full(SSD_CONV, SSD_XBC), full(1, SSD_XBC),
            full(1, LANES), full(1, LANES), full(nh, 1), full(nh, 1),
            full(1, SSD_WIDTH), full(1, SSD_WIDTH),
        ],
        out_specs=pl.BlockSpec((L, SSD_WIDTH), lambda c: (c, 0)),
        out_shape=jax.ShapeDtypeStruct((s, SSD_WIDTH), BF16),
        scratch_shapes=[
            pltpu.VMEM((SSD_TAIL + L, SSD_XBC), F32),
            pltpu.VMEM((L, SSD_XBC), F32),
            pltpu.VMEM((SSD_WIDTH, SSD_STATE), F32),
            pltpu.VMEM((L, SSD_WIDTH), F32),
        ],
        compiler_params=_cparams(("arbitrary",)),
        name="ssd_scan",
    )(xbc, z, dt, dtt, w_conv, b_conv.reshape(1, SSD_XBC), pad(dt_bias), pad(a_log),
      dt_bias.reshape(nh, 1), a_log.reshape(nh, 1),
      jnp.repeat(d_skip, SSD_HEAD_DIM).reshape(1, SSD_WIDTH), norm.reshape(1, SSD_WIDTH))


HALO = 16


def _merge_kernel(ya_ref, ys_ref, cb_ref, cc_ref, cx_ref, hc_ref, hx_ref, ga_ref, gs_ref, gc_ref,
                  wsc_ref, wa_ref, ws_ref, wc_ref, o_ref, ext_scr, yc_scr):
    i = pl.program_id(0)
    j = pl.program_id(1)
    tm = ya_ref.shape[0]

    @pl.when(j == 0)
    def _():
        halo = hc_ref[...].astype(F32) * hx_ref[...].astype(F32)
        ext_scr[0:HALO, :] = jnp.where(i > 0, halo, 0.0)
        for r0 in range(0, tm, ROW_CHUNK):
            rows = slice(r0, r0 + ROW_CHUNK)
            ext_scr[HALO + r0:HALO + r0 + ROW_CHUNK, :] = cc_ref[rows, :].astype(F32) * cx_ref[rows, :].astype(F32)
        for r0 in range(0, tm, ROW_CHUNK):
            rows = slice(r0, r0 + ROW_CHUNK)
            acc = jnp.zeros((ROW_CHUNK, CONV_WIDTH), F32)
            for k in range(CONV_K):
                start = HALO + r0 - (CONV_K - 1) + k
                acc = acc + wsc_ref[k:k + 1, :] * ext_scr[start:start + ROW_CHUNK, :]
            yc_scr[rows, :] = (cb_ref[rows, :].astype(F32) * acc).astype(BF16)

    merged = (_sigmoid(ga_ref[...].astype(F32)) * _dot(ya_ref[...], wa_ref[...].astype(BF16))
              + _sigmoid(gs_ref[...].astype(F32)) * _dot(ys_ref[...], ws_ref[...].astype(BF16))
              + _sigmoid(gc_ref[...].astype(F32)) * _dot(yc_scr[...], wc_ref[...].astype(BF16)))
    o_ref[...] = merged.astype(o_ref.dtype)


def _branch_merge(y_attn, y_ssd, tail, w_sc, w_a, w_s, w_c, lead, tm=1024, tn=256):
    s = y_attn.shape[0]
    d = w_a.shape[-1]
    tm = min(tm, s)
    cw = CONV_WIDTH
    g0 = 3 * cw // tn
    gd = d // tn
    row = lambda width, cb: pl.BlockSpec((tm, width), lambda i, j: (i, cb))
    halo = lambda cb: pl.BlockSpec((HALO, cw), lambda i, j: (jnp.maximum(i * (tm // HALO) - 1, 0), cb))
    gate = lambda k: pl.BlockSpec((tm, tn), lambda i, j: (i, g0 + k * gd + j))
    wcol = lambda kdim: _wspec(lead, (kdim, tn), lambda i, j: (0, j))
    return pl.pallas_call(
        _merge_kernel,
        grid=(s // tm, d // tn),
        in_specs=[
            row(ATTN_WIDTH, 0), row(SSD_WIDTH, 0),
            row(cw, 0), row(cw, 1), row(cw, 2), halo(1), halo(2),
            gate(0), gate(1), gate(2),
            pl.BlockSpec((CONV_K, cw), lambda i, j: (0, 0)),
            wcol(ATTN_WIDTH), wcol(SSD_WIDTH), wcol(cw),
        ],
        out_specs=pl.BlockSpec((tm, tn), lambda i, j: (i, j)),
        out_shape=jax.ShapeDtypeStruct((s, d), BF16),
        scratch_shapes=[pltpu.VMEM((HALO + tm, cw), F32), pltpu.VMEM((tm, cw), BF16)],
        compiler_params=_cparams(("arbitrary", "arbitrary")),
        name="branch_merge",
    )(y_attn, y_ssd, tail, tail, tail, tail, tail, tail, tail, tail, w_sc, w_a, w_s, w_c)


def _outproj_kernel(m_ref, w_ref, x_ref, gate_ref, o_ref):
    o_ref[...] = x_ref[...] + gate_ref[...] * _dot(m_ref[...], w_ref[...].astype(BF16))


def _out_proj(merged, w, lead, x, gate, tm=1024, tn=512):
    s, d = x.shape
    tm = min(tm, s)
    return pl.pallas_call(
        _outproj_kernel,
        grid=(s // tm, d // tn),
        in_specs=[
            pl.BlockSpec((tm, merged.shape[1]), lambda i, j: (i, 0)),
            _wspec(lead, (merged.shape[1], tn), lambda i, j: (0, j)),
            pl.BlockSpec((tm, tn), lambda i, j: (i, j)),
            pl.BlockSpec((1, tn), lambda i, j: (0, j)),
        ],
        out_specs=pl.BlockSpec((tm, tn), lambda i, j: (i, j)),
        out_shape=jax.ShapeDtypeStruct((s, d), F32),
        compiler_params=_cparams(("arbitrary", "arbitrary")),
        name="out_proj",
    )(merged, w, x, gate)


def _token_mix(x, gain, scale, shift, gate, l, w_mix_in, qk_norm, rel_bias, w_ssd_conv, b_ssd_conv,
               ssd_dt_bias, ssd_a_log, ssd_d, ssd_norm, w_sc_conv, w_br_attn, w_br_ssd, w_br_conv,
               w_mix_out):
    d = x.shape[1]
    lead = (l,)
    c_v = 2 * ATTN_WIDTH
    c_z = c_v + ATTN_WIDTH
    c_xbc = c_z + SSD_WIDTH
    c_dt = c_xbc + SSD_XBC
    c_tail = c_dt + SSD_HEADS
    n_tail = 3 * CONV_WIDTH + 3 * d
    assert w_mix_in.shape[-1] == c_tail + n_tail

    hgain = jnp.concatenate([jnp.tile(qk_norm[l, 0] * ATTN_Q_PRESCALE, ATTN_HEADS),
                             jnp.tile(qk_norm[l, 1], ATTN_HEADS)])[None]
    qk, h = _norm_qk_proj(x, gain, scale, shift, w_mix_in, lead, hgain)
    v = _proj(h, w_mix_in, lead, c_v, ATTN_WIDTH, BF16)
    z = _proj(h, w_mix_in, lead, c_z, SSD_WIDTH, BF16)
    xbc = _proj(h, w_mix_in, lead, c_xbc, SSD_XBC, BF16)
    w_dt = jnp.pad(w_mix_in[l, :, c_dt:c_tail], ((0, 0), (0, LANES - SSD_HEADS)))
    dt = _proj(h, w_dt, (), 0, LANES, F32)
    tail = _proj(h, w_mix_in[l, :, c_tail:], (), 0, n_tail, BF16)

    y_attn = _moba_attention(qk, v, rel_bias)
    y_ssd = _ssd(xbc, z, dt, w_ssd_conv[l], b_ssd_conv[l], ssd_dt_bias[l], ssd_a_log[l], ssd_d[l], ssd_norm[l])
    merged = _branch_merge(y_attn, y_ssd, tail, w_sc_conv[l], w_br_attn, w_br_ssd, w_br_conv, lead)
    return _out_proj(merged, w_mix_out, lead, x, gate)


def kernel(x, c, w_ada, b_ada, norm_gain, w_ffn_in, w_ffn_out, w_mix_in, qk_norm, rel_bias, w_ssd_conv, b_ssd_conv, ssd_dt_bias, ssd_a_log, ssd_d, ssd_norm, w_sc_conv, w_br_attn, w_br_ssd, w_br_conv, w_mix_out):
    b, s, d = x.shape
    depth = w_ada.shape[0]
    assert b == 1 and s % math.lcm(MOBA_BLOCK, SSD_CHUNK) == 0
    xs = x.reshape(s, d)
    ada = _ada_proj(c, w_ada, b_ada).reshape(depth, N_SUBLAYERS, 3, 1, d)
    for l in range(depth):
        mod = lambda i: (norm_gain[l, i][None], ada[l, i, 1], ada[l, i, 0], ada[l, i, 2])
        xs = _ffn(xs, *mod(0), w_ffn_in, w_ffn_out, (l, 0))
        xs = _token_mix(xs, *mod(1), l, w_mix_in, qk_norm, rel_bias, w_ssd_conv, b_ssd_conv,
                        ssd_dt_bias, ssd_a_log, ssd_d, ssd_norm, w_sc_conv,
                        w_br_attn, w_br_ssd, w_br_conv, w_mix_out)
        xs = _ffn(xs, *mod(2), w_ffn_in, w_ffn_out, (l, 1))
    return xs.reshape(b, s, d)
```

```python
import functools
import math

import jax
import jax.numpy as jnp
from jax import lax
from jax.experimental import pallas as pl
from jax.experimental.pallas import tpu as pltpu

F32 = jnp.float32
BF16 = jnp.bfloat16

ATTN_HEADS = 8
ATTN_HEAD_DIM = 128
ATTN_WIDTH = ATTN_HEADS * ATTN_HEAD_DIM
MOBA_BLOCK = 256
MOBA_TOPK = 3
REL_BUCKETS = 32
REL_MAX_DIST = 128
SSD_HEADS = 32
SSD_HEAD_DIM = 64
SSD_WIDTH = SSD_HEADS * SSD_HEAD_DIM
SSD_GROUPS = 4
SSD_STATE = 128
SSD_CONV = 4
SSD_CHUNK = 256
SSD_BC = SSD_GROUPS * SSD_STATE
SSD_XBC = SSD_WIDTH + 2 * SSD_BC
CONV_WIDTH = 1024
CONV_K = 3
N_SUBLAYERS = 3
FFN_RESIDUAL = 0.5
NORM_EPS = 1e-6
NEG_INF = -1e30

LANES = 128
VMEM_LIMIT_BYTES = 58 * 1024 * 1024


def _cparams(semantics):
    return pltpu.CompilerParams(dimension_semantics=semantics, vmem_limit_bytes=VMEM_LIMIT_BYTES)


def _sigmoid(x):
    return 1.0 / (1.0 + jnp.exp(-x))


def _silu(x):
    return x * _sigmoid(x)


def _dot(a, b):
    return jnp.dot(a, b, preferred_element_type=F32)


def _dot_nt(a, b):
    return lax.dot_general(a, b, (((1,), (1,)), ((), ())), preferred_element_type=F32)


def _wspec(lead, block, index_map):
    return pl.BlockSpec((None,) * len(lead) + block, lambda i, j: lead + index_map(i, j))


def _mod_norm(x, gain, scale, shift):
    y = x * lax.rsqrt(jnp.mean(x * x, axis=-1, keepdims=True) + NORM_EPS)
    return (y * gain) * (1.0 + scale) + shift


def _ada_kernel(c_ref, w_ref, b_ref, o_ref):
    cond = _silu(c_ref[...]).astype(BF16)
    o_ref[...] = _dot(cond, w_ref[...].astype(BF16)) + b_ref[...]


def _ada_proj(c, w_ada, b_ada, tn=1024):
    depth, d, n = w_ada.shape
    c8 = jnp.broadcast_to(c, (8, d))
    out = pl.pallas_call(
        _ada_kernel,
        grid=(depth, n // tn),
        in_specs=[
            pl.BlockSpec((8, d), lambda l, j: (0, 0)),
            pl.BlockSpec((None, d, tn), lambda l, j: (l, 0, j)),
            pl.BlockSpec((None, 1, tn), lambda l, j: (l, 0, j)),
        ],
        out_specs=pl.BlockSpec((None, 8, tn), lambda l, j: (l, 0, j)),
        out_shape=jax.ShapeDtypeStruct((depth, 8, n), F32),
        compiler_params=_cparams(("arbitrary", "arbitrary")),
        name="ada_proj",
    )(c8, w_ada, b_ada.reshape(depth, 1, n))
    return out[:, 0, :]


ROW_CHUNK = 128
COL_CHUNK = 512


def _ffn_kernel(x_ref, gain_ref, scale_ref, shift_ref, gate_ref, wg_ref, wu_ref, wo_ref,
                o_ref, h_scr):
    j = pl.program_id(1)
    tm, d = x_ref.shape

    @pl.when(j == 0)
    def _():
        def body(r, carry):
            rows = pl.ds(pl.multiple_of(r * ROW_CHUNK, ROW_CHUNK), ROW_CHUNK)
            h = _mod_norm(x_ref[rows, :], gain_ref[...], scale_ref[...], shift_ref[...])
            h_scr[rows, :] = h.astype(BF16)
            o_ref[rows, :] = jnp.zeros((ROW_CHUNK, d), F32)
            return carry
        lax.fori_loop(0, tm // ROW_CHUNK, body, 0)

    h = h_scr[...]
    g = _dot(h, wg_ref[...].astype(BF16))
    u = _dot(h, wu_ref[...].astype(BF16))
    a = (_silu(g) * u).astype(BF16)
    for c in range(0, d, COL_CHUNK):
        o_ref[:, c:c + COL_CHUNK] += _dot(a, wo_ref[:, c:c + COL_CHUNK].astype(BF16))

    @pl.when(j == pl.num_programs(1) - 1)
    def _():
        def body(r, carry):
            rows = pl.ds(pl.multiple_of(r * ROW_CHUNK, ROW_CHUNK), ROW_CHUNK)
            o_ref[rows, :] = x_ref[rows, :] + (FFN_RESIDUAL * gate_ref[...]) * o_ref[rows, :]
            return carry
        lax.fori_loop(0, tm // ROW_CHUNK, body, 0)


def _ffn(x, gain, scale, shift, gate, w_in, w_out, lead, tm=1024, tf=256):
    s, d = x.shape
    f = w_out.shape[-2]
    tm = min(tm, s)
    nf = f // tf
    vec = pl.BlockSpec((1, d), lambda i, j: (0, 0))
    return pl.pallas_call(
        _ffn_kernel,
        grid=(s // tm, nf),
        in_specs=[
            pl.BlockSpec((tm, d), lambda i, j: (i, 0)),
            vec, vec, vec, vec,
            _wspec(lead, (d, tf), lambda i, j: (0, j)),
            _wspec(lead, (d, tf), lambda i, j: (0, j + nf)),
            _wspec(lead, (tf, d), lambda i, j: (j, 0)),
        ],
        out_specs=pl.BlockSpec((tm, d), lambda i, j: (i, 0)),
        out_shape=jax.ShapeDtypeStruct((s, d), F32),
        scratch_shapes=[pltpu.VMEM((tm, d), BF16)],
        compiler_params=_cparams(("arbitrary", "arbitrary")),
        name="ffn",
    )(x, gain, scale, shift, gate, w_in, w_in, w_out)


def _head_rms_norm(y, gain):
    outs = []
    for c in range(0, y.shape[1], ATTN_HEAD_DIM):
        yc = y[:, c:c + ATTN_HEAD_DIM]
        outs.append(yc * lax.rsqrt(jnp.mean(yc * yc, axis=-1, keepdims=True) + NORM_EPS))
    return jnp.concatenate(outs, axis=1) * gain


def _normproj_kernel(x_ref, gain_ref, scale_ref, shift_ref, w_ref, hgain_ref, o_ref, h_ref):
    j = pl.program_id(1)
    tm = x_ref.shape[0]

    @pl.when(j == 0)
    def _():
        def body(r, carry):
            rows = pl.ds(pl.multiple_of(r * ROW_CHUNK, ROW_CHUNK), ROW_CHUNK)
            h = _mod_norm(x_ref[rows, :], gain_ref[...], scale_ref[...], shift_ref[...])
            h_ref[rows, :] = h.astype(BF16)
            return carry
        lax.fori_loop(0, tm // ROW_CHUNK, body, 0)

    y = _dot(h_ref[...], w_ref[...].astype(BF16))
    o_ref[...] = _head_rms_norm(y, hgain_ref[...]).astype(o_ref.dtype)


def _norm_qk_proj(x, gain, scale, shift, w, lead, hgain, tm=1024, tn=512):
    s, d = x.shape
    n = 2 * ATTN_WIDTH
    tm = min(tm, s)
    vec = pl.BlockSpec((1, d), lambda i, j: (0, 0))
    return pl.pallas_call(
        _normproj_kernel,
        grid=(s // tm, n // tn),
        in_specs=[
            pl.BlockSpec((tm, d), lambda i, j: (i, 0)),
            vec, vec, vec,
            _wspec(lead, (d, tn), lambda i, j: (0, j)),
            pl.BlockSpec((1, tn), lambda i, j: (0, j)),
        ],
        out_specs=[
            pl.BlockSpec((tm, tn), lambda i, j: (i, j)),
            pl.BlockSpec((tm, d), lambda i, j: (i, 0)),
        ],
        out_shape=[jax.ShapeDtypeStruct((s, n), BF16), jax.ShapeDtypeStruct((s, d), BF16)],
        compiler_params=_cparams(("arbitrary", "arbitrary")),
        name="norm_qk_proj",
    )(x, gain, scale, shift, w, hgain)


def _proj_kernel(h_ref, w_ref, o_ref):
    o_ref[...] = _dot(h_ref[...], w_ref[...].astype(BF16)).astype(o_ref.dtype)


def _proj(h, w, lead, col0, n, out_dtype, tm=1024, tn=512):
    s, d = h.shape
    tm = min(tm, s)
    tn = min(tn, n)
    off = col0 // tn
    return pl.pallas_call(
        _proj_kernel,
        grid=(s // tm, n // tn),
        in_specs=[
            pl.BlockSpec((tm, d), lambda i, j: (i, 0)),
            _wspec(lead, (d, tn), lambda i, j: (0, j + off)),
        ],
        out_specs=pl.BlockSpec((tm, tn), lambda i, j: (i, j)),
        out_shape=jax.ShapeDtypeStruct((s, n), out_dtype),
        compiler_params=_cparams(("arbitrary", "arbitrary")),
        name="proj",
    )(h, w)


def _proj_shift_kernel(h_ref, wa_ref, wb_ref, o_ref, w_scr, *, shift):
    d, tn = wa_ref.shape

    @pl.when(pl.program_id(1) == 0)
    def _():
        def body(r, carry):
            rows = pl.ds(pl.multiple_of(r * ROW_CHUNK, ROW_CHUNK), ROW_CHUNK)
            w = jnp.concatenate([wa_ref[rows, :], wb_ref[rows, :]], axis=1)
            w_scr[rows, :] = w[:, shift:shift + tn].astype(BF16)
            return carry
        lax.fori_loop(0, d // ROW_CHUNK, body, 0)

    o_ref[...] = _dot(h_ref[...], w_scr[...]).astype(o_ref.dtype)


def _proj_unaligned(h, w, lead, col0, n, out_dtype, tm=1024, tn=512):
    s, d = h.shape
    tm = min(tm, s)
    base = col0 // LANES * LANES
    shift = col0 - base
    assert 0 < shift < LANES and base % tn == 0 and n % tn == 0
    return pl.pallas_call(
        functools.partial(_proj_shift_kernel, shift=shift),
        grid=(n // tn, s // tm),
        in_specs=[
            pl.BlockSpec((tm, d), lambda j, i: (i, 0)),
            _wspec(lead, (d, tn), lambda j, i: (0, base // tn + j)),
            _wspec(lead, (d, LANES), lambda j, i: (0, (base + (j + 1) * tn) // LANES)),
        ],
        out_specs=pl.BlockSpec((tm, tn), lambda j, i: (i, j)),
        out_shape=jax.ShapeDtypeStruct((s, n), out_dtype),
        scratch_shapes=[pltpu.VMEM((d, tn), BF16)],
        compiler_params=_cparams(("arbitrary", "arbitrary")),
        name="proj_unaligned",
    )(h, w, w)


def _t5_bucket(dist):
    n = jnp.maximum(dist, 0)
    max_exact = REL_BUCKETS // 2
    ratio = jnp.log(jnp.maximum(n, 1).astype(F32) / max_exact) / math.log(REL_MAX_DIST / max_exact)
    large = max_exact + (ratio * (REL_BUCKETS - max_exact)).astype(jnp.int32)
    large = jnp.minimum(large, REL_BUCKETS - 1)
    return jnp.where(n < max_exact, n, large)


ATTN_VIS_STEPS = (8, 16, 24, 32)
ATTN_QK_AHEAD = 5
ATTN_SUM_ROWS = 16
LOG2E = 1.4426950408889634
ATTN_Q_PRESCALE = ATTN_HEAD_DIM ** -0.5 * LOG2E


def _moba_kernel(rel_ref, bkt_own_ref, bkt_prev_ref, q_ref, k_ref, v_ref, o_ref,
                 kmean_scr, bias_own_scr, bias_prev_scr, vt_scr, mask_scr):
    h = pl.program_id(0)
    i = pl.program_id(1)
    blk, hd = MOBA_BLOCK, ATTN_HEAD_DIM
    nb = k_ref.shape[0] // blk

    @pl.when(i == 0)
    def _():
        kmean_scr[...] = jnp.zeros(kmean_scr.shape, F32)

        ones_row = jnp.where(lax.broadcasted_iota(jnp.int32, (ATTN_SUM_ROWS, blk), 0) == 0, 1.0, 0.0)

        def block_body(b, carry):
            rows = pl.ds(pl.multiple_of(b * blk, blk), blk)
            kmean_scr[pl.ds(b, 1), :] = jnp.mean(k_ref[rows, :].astype(F32), axis=0, keepdims=True)
            vt_scr[b, :hd, :] = v_ref[rows, :].astype(F32).T.astype(BF16)
            vt_scr[b, hd:, :] = ones_row.astype(BF16)
            return carry
        lax.fori_loop(0, nb, block_body, 0)

        key = lax.broadcasted_iota(jnp.int32, (blk, blk), 0)
        qry = lax.broadcasted_iota(jnp.int32, (blk, blk), 1)
        bias_own_scr[...] = jnp.where(key <= qry, 0.0, NEG_INF)
        bias_prev_scr[...] = jnp.zeros((blk, blk), F32)

        def bias_body(b, carry):
            val = rel_ref[b, h] * LOG2E
            bias_own_scr[...] += jnp.where(bkt_own_ref[...] == b, val, 0.0)
            bias_prev_scr[...] += jnp.where(bkt_prev_ref[...] == b, val, 0.0)
            return carry
        lax.fori_loop(0, REL_BUCKETS, bias_body, 0)

    q = q_ref[...]

    km = kmean_scr[...]
    km_hi = km.astype(BF16)
    km_lo = (km - km_hi.astype(F32)).astype(BF16)
    gate = _dot_nt(km_hi, q) + _dot_nt(km_lo, q)
    rowb = lax.broadcasted_iota(jnp.int32, gate.shape, 0)
    rowf = rowb.astype(F32)
    gate = jnp.where(rowb < i, gate, -jnp.inf)
    sel = jnp.zeros(gate.shape, F32)
    for _ in range(min(MOBA_TOPK, nb)):
        top = jnp.max(gate, axis=0, keepdims=True)
        idx = jnp.min(jnp.where(gate == top, rowf, float(LANES)), axis=0, keepdims=True)
        hit = rowf == idx
        sel = jnp.where(hit & (top > -jnp.inf), 1.0, sel)
        gate = jnp.where(hit, -jnp.inf, gate)
    mask_scr[...] = jnp.where(sel > 0.0, 0.0, NEG_INF)

    far_bias = rel_ref[REL_BUCKETS - 1, h] * LOG2E
    visible = i + 1
    lo = 0
    for n_vis in sorted({min(v, nb) for v in ATTN_VIS_STEPS} | {nb}):
        @pl.when((visible > lo) & (visible <= n_vis))
        def _(n_vis=n_vis):
            _moba_tile(n_vis, i, far_bias, q, k_ref, o_ref, bias_own_scr, bias_prev_scr, vt_scr, mask_scr)
        lo = n_vis


def _moba_tile(n_vis, i, far_bias, q, k_ref, o_ref, bias_own_scr, bias_prev_scr, vt_scr, mask_scr):
    blk, hd = MOBA_BLOCK, ATTN_HEAD_DIM
    qt = q.astype(F32).T.astype(BF16)

    def rows(j):
        return pl.ds(pl.multiple_of(j * blk, blk), blk)

    n_far = n_vis - 2
    jp = jnp.maximum(i - 1, 0)
    s_own = _dot(k_ref[rows(i), :], qt)
    s_prev = _dot(k_ref[rows(jp), :], qt) if n_vis > 1 else None
    far_s = {j: _dot(k_ref[j * blk:(j + 1) * blk, :], qt) for j in range(min(ATTN_QK_AHEAD, n_far))}

    t = s_own + bias_own_scr[...]
    m = jnp.max(t, axis=0, keepdims=True)
    acc = _dot(vt_scr[i], jnp.exp2(t - m).astype(BF16))

    if n_vis > 1:
        chosen = jnp.where(i >= 1, mask_scr[pl.ds(jp, 1), :], NEG_INF)
        t = s_prev + bias_prev_scr[...] + chosen
        m_new = jnp.maximum(m, jnp.max(t, axis=0, keepdims=True))
        acc = jnp.exp2(m - m_new) * acc + _dot(vt_scr[jp], jnp.exp2(t - m_new).astype(BF16))
        m = m_new

    for j in range(n_far):
        shift = far_bias + jnp.where(j < i - 1, mask_scr[j:j + 1, :], NEG_INF)
        s = far_s.pop(j)
        ja = j + ATTN_QK_AHEAD
        if ja < n_far:
            far_s[ja] = _dot(k_ref[ja * blk:(ja + 1) * blk, :], qt)
        m_new = jnp.maximum(m, jnp.max(s, axis=0, keepdims=True) + shift)
        p = jnp.exp2(s + (shift - m_new))
        acc = jnp.exp2(m - m_new) * acc + _dot(vt_scr[j], p.astype(BF16))
        m = m_new

    o_ref[...] = (acc[:hd, :] / acc[hd:hd + 1, :]).T.astype(o_ref.dtype)


def _moba_attention(qk, v, rel_bias):
    s = qk.shape[0]
    blk, hd, nh = MOBA_BLOCK, ATTN_HEAD_DIM, ATTN_HEADS
    assert s % blk == 0 and s // blk < LANES
    assert blk >= REL_MAX_DIST
    r = jnp.arange(blk)
    dist_own = r[None, :] - r[:, None]
    bkt_own = _t5_bucket(dist_own)
    bkt_prev = _t5_bucket(dist_own + blk)
    const = lambda h, i: (0, 0)
    return pl.pallas_call(
        _moba_kernel,
        grid=(nh, s // blk),
        in_specs=[
            pl.BlockSpec(memory_space=pltpu.SMEM),
            pl.BlockSpec((blk, blk), const),
            pl.BlockSpec((blk, blk), const),
            pl.BlockSpec((blk, hd), lambda h, i: (i, h)),
            pl.BlockSpec((s, hd), lambda h, i: (0, nh + h)),
            pl.BlockSpec((s, hd), lambda h, i: (0, h)),
        ],
        out_specs=pl.BlockSpec((blk, hd), lambda h, i: (i, h)),
        out_shape=jax.ShapeDtypeStruct((s, nh * hd), BF16),
        scratch_shapes=[
            pltpu.VMEM((LANES, hd), F32),
            pltpu.VMEM((blk, blk), F32),
            pltpu.VMEM((blk, blk), F32),
            pltpu.VMEM((s // blk, hd + ATTN_SUM_ROWS, blk), BF16),
            pltpu.VMEM((LANES, blk), F32),
        ],
        compiler_params=_cparams(("arbitrary", "arbitrary")),
        name="moba_attention",
    )(rel_bias, bkt_own, bkt_prev, qk, qk, v)


SSD_TAIL = 8
SSD_CONV_COLS = 512


def _softplus(x):
    return jnp.maximum(x, 0.0) + jnp.log(1.0 + jnp.exp(-jnp.abs(x)))


def _split3(x):
    hi = x.astype(BF16)
    r1 = x - hi.astype(F32)
    mid = r1.astype(BF16)
    lo = (r1 - mid.astype(F32)).astype(BF16)
    return hi, mid, lo


def _ssd_kernel(xbc_ref, z_ref, dt_ref, dtt_ref, wconv_ref, bconv_ref, dtb_ref, alog_ref,
                dtbt_ref, alogt_ref, dskip_ref, norm_ref, y_ref,
                tail_scr, xact_scr, state_scr):
    c = pl.program_id(0)
    L, P, N, W = SSD_CHUNK, SSD_HEAD_DIM, SSD_STATE, SSD_WIDTH
    assert P & (P - 1) == 0

    @pl.when(c == 0)
    def _():
        tail_scr[...] = jnp.zeros(tail_scr.shape, F32)
        state_scr[...] = jnp.zeros(state_scr.shape, F32)

    rr = lax.broadcasted_iota(jnp.int32, (L, L), 0)
    cc = lax.broadcasted_iota(jnp.int32, (L, L), 1)
    causal = rr >= cc

    shifts = [jnp.where(rr - cc == k, 1.0, 0.0).astype(BF16) for k in range(1, SSD_CONV)]
    row8 = lax.broadcasted_iota(jnp.int32, (SSD_TAIL, SSD_CONV_COLS), 0)
    for c0 in range(0, SSD_XBC, SSD_CONV_COLS):
        cols = slice(c0, c0 + SSD_CONV_COLS)
        x = xbc_ref[:, cols]
        tail = tail_scr[:, cols]
        acc = bconv_ref[:, cols] + wconv_ref[SSD_CONV - 1:SSD_CONV, cols] * x.astype(F32)
        fix = jnp.zeros((SSD_TAIL, SSD_CONV_COLS), F32)
        for k in range(1, SSD_CONV):
            wk = wconv_ref[SSD_CONV - 1 - k:SSD_CONV - k, cols]
            acc = acc + wk * _dot(shifts[k - 1], x)
            fix = fix + wk * jnp.where(row8 < k, pltpu.roll(tail, k, 0), 0.0)
        acc = jnp.concatenate([acc[:SSD_TAIL] + fix, acc[SSD_TAIL:]], axis=0)
        xact_scr[:, cols] = _silu(acc)
        tail_scr[:, cols] = x[L - SSD_TAIL:, :].astype(F32)

    dtv = _softplus(dt_ref[...] + dtb_ref[...])
    ad = dtv * (-jnp.exp(alog_ref[...]))
    dtt = _softplus(dtt_ref[...] + dtbt_ref[...])
    adt = dtt * (-jnp.exp(alogt_ref[...]))
    lower = jnp.where(causal, 1.0, 0.0).astype(BF16)
    upper = jnp.where(rr <= cc, 1.0, 0.0).astype(BF16)
    a_cs = sum(_dot(lower, t) for t in _split3(ad))
    a_cst = sum(_dot(t, upper) for t in _split3(adt))
    last = a_cs[L - 1:L, :]

    factors = jnp.concatenate([dtv, jnp.exp(a_cs), jnp.exp(last - a_cs)], axis=0).astype(BF16)
    group_w = W // SSD_GROUPS
    heads_per_group = SSD_HEADS // SSD_GROUPS
    e_row = lax.broadcasted_iota(jnp.int32, (LANES, group_w), 0)
    e_head = lax.shift_right_logical(lax.broadcasted_iota(jnp.int32, (LANES, group_w), 1),
                                     P.bit_length() - 1)
    pair_lane = lax.broadcasted_iota(jnp.int32, (L, 2 * P), 1)

    for g in range(SSD_GROUPS):
        gc = slice(g * group_w, (g + 1) * group_w)
        expand = jnp.where(e_row == g * heads_per_group + e_head, 1.0, 0.0).astype(BF16)
        spread = _dot(factors, expand)
        xs = xact_scr[:, gc]
        xd = xs * spread[:L]
        xd_b = xd.astype(BF16)
        xe_b = (xd * spread[2 * L:]).astype(BF16)
        bg = xact_scr[:, W + g * N:W + (g + 1) * N].astype(BF16)
        cg = xact_scr[:, W + SSD_BC + g * N:W + SSD_BC + (g + 1) * N].astype(BF16)
        st = state_scr[gc, :]
        y_off = _dot_nt(cg, st.astype(BF16)) * spread[L:2 * L]
        new = lax.dot_general(xe_b, bg, (((0,), (0,)), ((), ())), preferred_element_type=F32)
        cb = _dot_nt(cg, bg)
        pairs = []
        for r in range(0, heads_per_group, 2):
            h = g * heads_per_group + r
            xd_pair = xd_b[:, r * P:(r + 2) * P]
            halves = []
            for hh in (h, h + 1):
                dec = jnp.exp(jnp.where(causal, a_cs[:, hh:hh + 1] - a_cst[hh:hh + 1, :], -jnp.inf))
                halves.append(_dot((cb * dec).astype(BF16), xd_pair))
                hr = slice((hh - g * heads_per_group) * P, (hh - g * heads_per_group + 1) * P)
                state_scr[g * group_w + hr.start:g * group_w + hr.stop, :] = (
                    jnp.exp(a_cst[hh:hh + 1, L - 1:L]) * st[hr, :] + new[hr, :])
            pairs.append(jnp.where(pair_lane < P, halves[0], halves[1]))
        y = jnp.concatenate(pairs, axis=1) + y_off + xs * dskip_ref[:, gc]
        y = y * _silu(z_ref[:, gc].astype(F32))
        y = y * lax.rsqrt(jnp.mean(y * y, axis=-1, keepdims=True) + NORM_EPS)
        y_ref[:, gc] = (y * norm_ref[:, gc]).astype(y_ref.dtype)


def _ssd(xbc, z, dt, w_conv, b_conv, dt_bias, a_log, d_skip, norm):
    s = xbc.shape[0]
    L, nh = SSD_CHUNK, SSD_HEADS
    assert s % L == 0
    dtt = dt[:, :nh].T
    pad = lambda v: jnp.pad(v, (0, LANES - nh)).reshape(1, LANES)
    full = lambda r, cdim: pl.BlockSpec((r, cdim), lambda c: (0, 0))
    return pl.pallas_call(
        _ssd_kernel,
        grid=(s // L,),
        in_specs=[
            pl.BlockSpec((L, SSD_XBC), lambda c: (c, 0)),
            pl.BlockSpec((L, SSD_WIDTH), lambda c: (c, 0)),
            pl.BlockSpec((L, LANES), lambda c: (c, 0)),
            pl.BlockSpec((nh, L), lambda c: (0, c)),
            full(SSD_CONV, SSD_XBC), full(1, SSD_XBC),
            full(1, LANES), full(1, LANES), full(nh, 1), full(nh, 1),
            full(1, SSD_WIDTH), full(1, SSD_WIDTH),
        ],
        out_specs=pl.BlockSpec((L, SSD_WIDTH), lambda c: (c, 0)),
        out_shape=jax.ShapeDtypeStruct((s, SSD_WIDTH), BF16),
        scratch_shapes=[
            pltpu.VMEM((SSD_TAIL, SSD_XBC), F32),
            pltpu.VMEM((L, SSD_XBC), F32),
            pltpu.VMEM((SSD_WIDTH, SSD_STATE), F32),
        ],
        compiler_params=_cparams(("arbitrary",)),
        name="ssd_scan",
    )(xbc, z, dt, dtt, w_conv, b_conv.reshape(1, SSD_XBC), pad(dt_bias), pad(a_log),
      dt_bias.reshape(nh, 1), a_log.reshape(nh, 1),
      jnp.repeat(d_skip, SSD_HEAD_DIM).reshape(1, SSD_WIDTH), norm.reshape(1, SSD_WIDTH))


HALO = 16


def _merge_kernel(ya_ref, ys_ref, cb_ref, cc_ref, cx_ref, hc_ref, hx_ref, ga_ref, gs_ref, gc_ref,
                  wsc_ref, wa_ref, ws_ref, wc_ref, o_ref, ext_scr, yc_scr):
    i = pl.program_id(0)
    j = pl.program_id(1)
    tm = ya_ref.shape[0]

    @pl.when(j == 0)
    def _():
        halo = hc_ref[...].astype(F32) * hx_ref[...].astype(F32)
        ext_scr[0:HALO, :] = jnp.where(i > 0, halo, 0.0)
        for r0 in range(0, tm, ROW_CHUNK):
            rows = slice(r0, r0 + ROW_CHUNK)
            ext_scr[HALO + r0:HALO + r0 + ROW_CHUNK, :] = cc_ref[rows, :].astype(F32) * cx_ref[rows, :].astype(F32)
        for r0 in range(0, tm, ROW_CHUNK):
            rows = slice(r0, r0 + ROW_CHUNK)
            acc = jnp.zeros((ROW_CHUNK, CONV_WIDTH), F32)
            for k in range(CONV_K):
                start = HALO + r0 - (CONV_K - 1) + k
                acc = acc + wsc_ref[k:k + 1, :] * ext_scr[start:start + ROW_CHUNK, :]
            yc_scr[rows, :] = (cb_ref[rows, :].astype(F32) * acc).astype(BF16)

    merged = (_sigmoid(ga_ref[...].astype(F32)) * _dot(ya_ref[...], wa_ref[...].astype(BF16))
              + _sigmoid(gs_ref[...].astype(F32)) * _dot(ys_ref[...], ws_ref[...].astype(BF16))
              + _sigmoid(gc_ref[...].astype(F32)) * _dot(yc_scr[...], wc_ref[...].astype(BF16)))
    o_ref[...] = merged.astype(o_ref.dtype)


def _branch_merge(y_attn, y_ssd, tail, w_sc, w_a, w_s, w_c, lead, tm=1024, tn=256):
    s = y_attn.shape[0]
    d = w_a.shape[-1]
    tm = min(tm, s)
    cw = CONV_WIDTH
    g0 = 3 * cw // tn
    gd = d // tn
    row = lambda width, cb: pl.BlockSpec((tm, width), lambda i, j: (i, cb))
    halo = lambda cb: pl.BlockSpec((HALO, cw), lambda i, j: (jnp.maximum(i * (tm // HALO) - 1, 0), cb))
    gate = lambda k: pl.BlockSpec((tm, tn), lambda i, j: (i, g0 + k * gd + j))
    wcol = lambda kdim: _wspec(lead, (kdim, tn), lambda i, j: (0, j))
    return pl.pallas_call(
        _merge_kernel,
        grid=(s // tm, d // tn),
        in_specs=[
            row(ATTN_WIDTH, 0), row(SSD_WIDTH, 0),
            row(cw, 0), row(cw, 1), row(cw, 2), halo(1), halo(2),
            gate(0), gate(1), gate(2),
            pl.BlockSpec((CONV_K, cw), lambda i, j: (0, 0)),
            wcol(ATTN_WIDTH), wcol(SSD_WIDTH), wcol(cw),
        ],
        out_specs=pl.BlockSpec((tm, tn), lambda i, j: (i, j)),
        out_shape=jax.ShapeDtypeStruct((s, d), BF16),
        scratch_shapes=[pltpu.VMEM((HALO + tm, cw), F32), pltpu.VMEM((tm, cw), BF16)],
        compiler_params=_cparams(("arbitrary", "arbitrary")),
        name="branch_merge",
    )(y_attn, y_ssd, tail, tail, tail, tail, tail, tail, tail, tail, w_sc, w_a, w_s, w_c)


def _outproj_kernel(m_ref, w_ref, x_ref, gate_ref, o_ref):
    o_ref[...] = x_ref[...] + gate_ref[...] * _dot(m_ref[...], w_ref[...].astype(BF16))


def _out_proj(merged, w, lead, x, gate, tm=1024, tn=512):
    s, d = x.shape
    tm = min(tm, s)
    return pl.pallas_call(
        _outproj_kernel,
        grid=(s // tm, d // tn),
        in_specs=[
            pl.BlockSpec((tm, merged.shape[1]), lambda i, j: (i, 0)),
            _wspec(lead, (merged.shape[1], tn), lambda i, j: (0, j)),
            pl.BlockSpec((tm, tn), lambda i, j: (i, j)),
            pl.BlockSpec((1, tn), lambda i, j: (0, j)),
        ],
        out_specs=pl.BlockSpec((tm, tn), lambda i, j: (i, j)),
        out_shape=jax.ShapeDtypeStruct((s, d), F32),
        compiler_params=_cparams(("arbitrary", "arbitrary")),
        name="out_proj",
    )(merged, w, x, gate)


def _token_mix(x, gain, scale, shift, gate, l, w_mix_in, qk_norm, rel_bias, w_ssd_conv, b_ssd_conv,
               ssd_dt_bias, ssd_a_log, ssd_d, ssd_norm, w_sc_conv, w_br_attn, w_br_ssd, w_br_conv,
               w_mix_out):
    d = x.shape[1]
    lead = (l,)
    c_v = 2 * ATTN_WIDTH
    c_z = c_v + ATTN_WIDTH
    c_xbc = c_z + SSD_WIDTH
    c_dt = c_xbc + SSD_XBC
    c_tail = c_dt + SSD_HEADS
    n_tail = 3 * CONV_WIDTH + 3 * d
    assert w_mix_in.shape[-1] == c_tail + n_tail

    hgain = jnp.concatenate([jnp.tile(qk_norm[l, 0] * ATTN_Q_PRESCALE, ATTN_HEADS),
                             jnp.tile(qk_norm[l, 1], ATTN_HEADS)])[None]
    qk, h = _norm_qk_proj(x, gain, scale, shift, w_mix_in, lead, hgain)
    v = _proj(h, w_mix_in, lead, c_v, ATTN_WIDTH, BF16)
    z = _proj(h, w_mix_in, lead, c_z, SSD_WIDTH, BF16)
    xbc = _proj(h, w_mix_in, lead, c_xbc, SSD_XBC, BF16)
    w_dt = jnp.pad(w_mix_in[l, :, c_dt:c_tail], ((0, 0), (0, LANES - SSD_HEADS)))
    dt = _proj(h, w_dt, (), 0, LANES, F32)
    tail = _proj_unaligned(h, w_mix_in, lead, c_tail, n_tail, BF16)

    y_attn = _moba_attention(qk, v, rel_bias)
    y_ssd = _ssd(xbc, z, dt, w_ssd_conv[l], b_ssd_conv[l], ssd_dt_bias[l], ssd_a_log[l], ssd_d[l], ssd_norm[l])
    merged = _branch_merge(y_attn, y_ssd, tail, w_sc_conv[l], w_br_attn, w_br_ssd, w_br_conv, lead)
    return _out_proj(merged, w_mix_out, lead, x, gate)


def kernel(x, c, w_ada, b_ada, norm_gain, w_ffn_in, w_ffn_out, w_mix_in, qk_norm, rel_bias, w_ssd_conv, b_ssd_conv, ssd_dt_bias, ssd_a_log, ssd_d, ssd_norm, w_sc_conv, w_br_attn, w_br_ssd, w_br_conv, w_mix_out):
    b, s, d = x.shape
    depth = w_ada.shape[0]
    assert b == 1 and s % math.lcm(MOBA_BLOCK, SSD_CHUNK) == 0
    xs = x.reshape(s, d)
    ada = _ada_proj(c, w_ada, b_ada).reshape(depth, N_SUBLAYERS, 3, 1, d)
    for l in range(depth):
        mod = lambda i: (norm_gain[l, i][None], ada[l, i, 1], ada[l, i, 0], ada[l, i, 2])
        xs = _ffn(xs, *mod(0), w_ffn_in, w_ffn_out, (l, 0))
        xs = _token_mix(xs, *mod(1), l, w_mix_in, qk_norm, rel_bias, w_ssd_conv, b_ssd_conv,
                        ssd_dt_bias, ssd_a_log, ssd_d, ssd_norm, w_sc_conv,
                        w_br_attn, w_br_ssd, w_br_conv, w_mix_out)
        xs = _ffn(xs, *mod(2), w_ffn_in, w_ffn_out, (l, 1))
    return xs.reshape(b, s, d)
```

```python
import functools
import math

import jax
import jax.numpy as jnp
from jax import lax
from jax.experimental import pallas as pl
from jax.experimental.pallas import tpu as pltpu

F32 = jnp.float32
BF16 = jnp.bfloat16

ATTN_HEADS = 8
ATTN_HEAD_DIM = 128
ATTN_WIDTH = ATTN_HEADS * ATTN_HEAD_DIM
MOBA_BLOCK = 256
MOBA_TOPK = 3
REL_BUCKETS = 32
REL_MAX_DIST = 128
SSD_HEADS = 32
SSD_HEAD_DIM = 64
SSD_WIDTH = SSD_HEADS * SSD_HEAD_DIM
SSD_GROUPS = 4
SSD_STATE = 128
SSD_CONV = 4
SSD_CHUNK = 256
SSD_BC = SSD_GROUPS * SSD_STATE
SSD_XBC = SSD_WIDTH + 2 * SSD_BC
CONV_WIDTH = 1024
CONV_K = 3
N_SUBLAYERS = 3
FFN_RESIDUAL = 0.5
NORM_EPS = 1e-6
NEG_INF = -1e30

LANES = 128
VMEM_LIMIT_BYTES = 58 * 1024 * 1024


def _cparams(semantics):
    return pltpu.CompilerParams(dimension_semantics=semantics, vmem_limit_bytes=VMEM_LIMIT_BYTES)


def _sigmoid(x):
    return 1.0 / (1.0 + jnp.exp(-x))


def _silu(x):
    return x * _sigmoid(x)


def _dot(a, b):
    return jnp.dot(a, b, preferred_element_type=F32)


def _dot_nt(a, b):
    return lax.dot_general(a, b, (((1,), (1,)), ((), ())), preferred_element_type=F32)


def _wspec(lead, block, index_map):
    return pl.BlockSpec((None,) * len(lead) + block, lambda i, j: lead + index_map(i, j))


def _mod_norm(x, gain, scale, shift):
    y = x * lax.rsqrt(jnp.mean(x * x, axis=-1, keepdims=True) + NORM_EPS)
    return (y * gain) * (1.0 + scale) + shift


def _ada_kernel(c_ref, w_ref, b_ref, o_ref):
    cond = _silu(c_ref[...]).astype(BF16)
    o_ref[...] = _dot(cond, w_ref[...].astype(BF16)) + b_ref[...]


def _ada_proj(c, w_ada, b_ada, tn=1024):
    depth, d, n = w_ada.shape
    c8 = jnp.broadcast_to(c, (8, d))
    out = pl.pallas_call(
        _ada_kernel,
        grid=(depth, n // tn),
        in_specs=[
            pl.BlockSpec((8, d), lambda l, j: (0, 0)),
            pl.BlockSpec((None, d, tn), lambda l, j: (l, 0, j)),
            pl.BlockSpec((None, 1, tn), lambda l, j: (l, 0, j)),
        ],
        out_specs=pl.BlockSpec((None, 8, tn), lambda l, j: (l, 0, j)),
        out_shape=jax.ShapeDtypeStruct((depth, 8, n), F32),
        compiler_params=_cparams(("arbitrary", "arbitrary")),
        name="ada_proj",
    )(c8, w_ada, b_ada.reshape(depth, 1, n))
    return out[:, 0, :]


ROW_CHUNK = 128
COL_CHUNK = 512


def _ffn_kernel(x_ref, gain_ref, scale_ref, shift_ref, gate_ref, wg_ref, wu_ref, wo_ref,
                o_ref, h_scr):
    j = pl.program_id(1)
    tm, d = x_ref.shape

    @pl.when(j == 0)
    def _():
        def body(r, carry):
            rows = pl.ds(pl.multiple_of(r * ROW_CHUNK, ROW_CHUNK), ROW_CHUNK)
            h = _mod_norm(x_ref[rows, :], gain_ref[...], scale_ref[...], shift_ref[...])
            h_scr[rows, :] = h.astype(BF16)
            o_ref[rows, :] = jnp.zeros((ROW_CHUNK, d), F32)
            return carry
        lax.fori_loop(0, tm // ROW_CHUNK, body, 0)

    h = h_scr[...]
    g = _dot(h, wg_ref[...].astype(BF16))
    u = _dot(h, wu_ref[...].astype(BF16))
    a = (_silu(g) * u).astype(BF16)
    for c in range(0, d, COL_CHUNK):
        o_ref[:, c:c + COL_CHUNK] += _dot(a, wo_ref[:, c:c + COL_CHUNK].astype(BF16))

    @pl.when(j == pl.num_programs(1) - 1)
    def _():
        def body(r, carry):
            rows = pl.ds(pl.multiple_of(r * ROW_CHUNK, ROW_CHUNK), ROW_CHUNK)
            o_ref[rows, :] = x_ref[rows, :] + (FFN_RESIDUAL * gate_ref[...]) * o_ref[rows, :]
            return carry
        lax.fori_loop(0, tm // ROW_CHUNK, body, 0)


def _ffn(x, gain, scale, shift, gate, w_in, w_out, lead, tm=1024, tf=256):
    s, d = x.shape
    f = w_out.shape[-2]
    tm = min(tm, s)
    nf = f // tf
    vec = pl.BlockSpec((1, d), lambda i, j: (0, 0))
    return pl.pallas_call(
        _ffn_kernel,
        grid=(s // tm, nf),
        in_specs=[
            pl.BlockSpec((tm, d), lambda i, j: (i, 0)),
            vec, vec, vec, vec,
            _wspec(lead, (d, tf), lambda i, j: (0, j)),
            _wspec(lead, (d, tf), lambda i, j: (0, j + nf)),
            _wspec(lead, (tf, d), lambda i, j: (j, 0)),
        ],
        out_specs=pl.BlockSpec((tm, d), lambda i, j: (i, 0)),
        out_shape=jax.ShapeDtypeStruct((s, d), F32),
        scratch_shapes=[pltpu.VMEM((tm, d), BF16)],
        compiler_params=_cparams(("arbitrary", "arbitrary")),
        name="ffn",
    )(x, gain, scale, shift, gate, w_in, w_in, w_out)


def _head_rms_norm(y, gain):
    outs = []
    for c in range(0, y.shape[1], ATTN_HEAD_DIM):
        yc = y[:, c:c + ATTN_HEAD_DIM]
        outs.append(yc * lax.rsqrt(jnp.mean(yc * yc, axis=-1, keepdims=True) + NORM_EPS))
    return jnp.concatenate(outs, axis=1) * gain


def _normproj_kernel(x_ref, gain_ref, scale_ref, shift_ref, w_ref, hgain_ref, o_ref, h_ref):
    j = pl.program_id(1)
    tm = x_ref.shape[0]

    @pl.when(j == 0)
    def _():
        def body(r, carry):
            rows = pl.ds(pl.multiple_of(r * ROW_CHUNK, ROW_CHUNK), ROW_CHUNK)
            h = _mod_norm(x_ref[rows, :], gain_ref[...], scale_ref[...], shift_ref[...])
            h_ref[rows, :] = h.astype(BF16)
            return carry
        lax.fori_loop(0, tm // ROW_CHUNK, body, 0)

    y = _dot_nt(h_ref[...], w_ref[...].astype(BF16))
    o_ref[...] = _head_rms_norm(y, hgain_ref[...]).astype(o_ref.dtype)


def _norm_qk_proj(x, gain, scale, shift, wt, lead, hgain, tm=1024, tn=512):
    s, d = x.shape
    n = 2 * ATTN_WIDTH
    tm = min(tm, s)
    vec = pl.BlockSpec((1, d), lambda i, j: (0, 0))
    return pl.pallas_call(
        _normproj_kernel,
        grid=(s // tm, n // tn),
        in_specs=[
            pl.BlockSpec((tm, d), lambda i, j: (i, 0)),
            vec, vec, vec,
            _wspec(lead, (tn, d), lambda i, j: (j, 0)),
            pl.BlockSpec((1, tn), lambda i, j: (0, j)),
        ],
        out_specs=[
            pl.BlockSpec((tm, tn), lambda i, j: (i, j)),
            pl.BlockSpec((tm, d), lambda i, j: (i, 0)),
        ],
        out_shape=[jax.ShapeDtypeStruct((s, n), BF16), jax.ShapeDtypeStruct((s, d), BF16)],
        compiler_params=_cparams(("arbitrary", "arbitrary")),
        name="norm_qk_proj",
    )(x, gain, scale, shift, wt, hgain)


def _proj_kernel(h_ref, *refs, shift):
    wa_ref, o_ref, w_scr = refs[0], refs[-2], refs[-1]
    tn = wa_ref.shape[0]

    @pl.when(pl.program_id(1) == 0)
    def _():
        w_scr[:tn - shift, :] = wa_ref[shift:, :].astype(BF16)
        if shift:
            w_scr[tn - shift:, :] = refs[1][:shift, :].astype(BF16)

    o_ref[...] = _dot_nt(h_ref[...], w_scr[...]).astype(o_ref.dtype)


PROJ_SHIFT_ROWS = 128


def _proj(h, wt, lead, row0, n, out_dtype, tm=1024, tn=512):
    s, d = h.shape
    tm = min(tm, s)
    tn = min(tn, n)
    base = row0 // tn * tn
    shift = row0 - base
    assert n % tn == 0 and shift % 16 == 0 and shift <= PROJ_SHIFT_ROWS and tn % PROJ_SHIFT_ROWS == 0
    w_specs = [_wspec(lead, (tn, d), lambda j, i: (base // tn + j, 0))]
    if shift:
        w_specs.append(_wspec(lead, (PROJ_SHIFT_ROWS, d),
                              lambda j, i: ((base + (j + 1) * tn) // PROJ_SHIFT_ROWS, 0)))
    return pl.pallas_call(
        functools.partial(_proj_kernel, shift=shift),
        grid=(n // tn, s // tm),
        in_specs=[pl.BlockSpec((tm, d), lambda j, i: (i, 0))] + w_specs,
        out_specs=pl.BlockSpec((tm, tn), lambda j, i: (i, j)),
        out_shape=jax.ShapeDtypeStruct((s, n), out_dtype),
        scratch_shapes=[pltpu.VMEM((tn, d), BF16)],
        compiler_params=_cparams(("arbitrary", "arbitrary")),
        name="proj",
    )(h, *([wt] * len(w_specs)))


def _t5_bucket(dist):
    n = jnp.maximum(dist, 0)
    max_exact = REL_BUCKETS // 2
    ratio = jnp.log(jnp.maximum(n, 1).astype(F32) / max_exact) / math.log(REL_MAX_DIST / max_exact)
    large = max_exact + (ratio * (REL_BUCKETS - max_exact)).astype(jnp.int32)
    large = jnp.minimum(large, REL_BUCKETS - 1)
    return jnp.where(n < max_exact, n, large)


ATTN_VIS_STEPS = (8, 16, 24, 32)
ATTN_QK_AHEAD = 5
ATTN_SUM_ROWS = 16
LOG2E = 1.4426950408889634
ATTN_Q_PRESCALE = ATTN_HEAD_DIM ** -0.5 * LOG2E


def _moba_kernel(rel_ref, bkt_own_ref, bkt_prev_ref, q_ref, k_ref, v_ref, o_ref,
                 kmean_scr, bias_own_scr, bias_prev_scr, vt_scr, mask_scr):
    h = pl.program_id(0)
    i = pl.program_id(1)
    blk, hd = MOBA_BLOCK, ATTN_HEAD_DIM
    nb = k_ref.shape[0] // blk

    @pl.when(i == 0)
    def _():
        kmean_scr[...] = jnp.zeros(kmean_scr.shape, F32)

        ones_row = jnp.where(lax.broadcasted_iota(jnp.int32, (ATTN_SUM_ROWS, blk), 0) == 0, 1.0, 0.0)

        def block_body(b, carry):
            rows = pl.ds(pl.multiple_of(b * blk, blk), blk)
            kmean_scr[pl.ds(b, 1), :] = jnp.mean(k_ref[rows, :].astype(F32), axis=0, keepdims=True)
            vt_scr[b, :hd, :] = v_ref[rows, :].astype(F32).T.astype(BF16)
            vt_scr[b, hd:, :] = ones_row.astype(BF16)
            return carry
        lax.fori_loop(0, nb, block_body, 0)

        key = lax.broadcasted_iota(jnp.int32, (blk, blk), 0)
        qry = lax.broadcasted_iota(jnp.int32, (blk, blk), 1)
        bias_own_scr[...] = jnp.where(key <= qry, 0.0, NEG_INF)
        bias_prev_scr[...] = jnp.zeros((blk, blk), F32)

        def bias_body(b, carry):
            val = rel_ref[b, h] * LOG2E
            bias_own_scr[...] += jnp.where(bkt_own_ref[...] == b, val, 0.0)
            bias_prev_scr[...] += jnp.where(bkt_prev_ref[...] == b, val, 0.0)
            return carry
        lax.fori_loop(0, REL_BUCKETS, bias_body, 0)

    q = q_ref[...]

    km = kmean_scr[...]
    km_hi = km.astype(BF16)
    km_lo = (km - km_hi.astype(F32)).astype(BF16)
    gate = _dot_nt(km_hi, q) + _dot_nt(km_lo, q)
    rowb = lax.broadcasted_iota(jnp.int32, gate.shape, 0)
    rowf = rowb.astype(F32)
    gate = jnp.where(rowb < i, gate, -jnp.inf)
    sel = jnp.zeros(gate.shape, F32)
    for _ in range(min(MOBA_TOPK, nb)):
        top = jnp.max(gate, axis=0, keepdims=True)
        idx = jnp.min(jnp.where(gate == top, rowf, float(LANES)), axis=0, keepdims=True)
        hit = rowf == idx
        sel = jnp.where(hit & (top > -jnp.inf), 1.0, sel)
        gate = jnp.where(hit, -jnp.inf, gate)
    mask_scr[...] = jnp.where(sel > 0.0, 0.0, NEG_INF)

    far_bias = rel_ref[REL_BUCKETS - 1, h] * LOG2E
    visible = i + 1
    lo = 0
    for n_vis in sorted({min(v, nb) for v in ATTN_VIS_STEPS} | {nb}):
        @pl.when((visible > lo) & (visible <= n_vis))
        def _(n_vis=n_vis):
            _moba_tile(n_vis, i, far_bias, q, k_ref, o_ref, bias_own_scr, bias_prev_scr, vt_scr, mask_scr)
        lo = n_vis


def _moba_tile(n_vis, i, far_bias, q, k_ref, o_ref, bias_own_scr, bias_prev_scr, vt_scr, mask_scr):
    blk, hd = MOBA_BLOCK, ATTN_HEAD_DIM
    qt = q.astype(F32).T.astype(BF16)

    def rows(j):
        return pl.ds(pl.multiple_of(j * blk, blk), blk)

    n_far = n_vis - 2
    jp = jnp.maximum(i - 1, 0)
    s_own = _dot(k_ref[rows(i), :], qt)
    s_prev = _dot(k_ref[rows(jp), :], qt) if n_vis > 1 else None
    far_s = {j: _dot(k_ref[j * blk:(j + 1) * blk, :], qt) for j in range(min(ATTN_QK_AHEAD, n_far))}

    t = s_own + bias_own_scr[...]
    m = jnp.max(t, axis=0, keepdims=True)
    acc = _dot(vt_scr[i], jnp.exp2(t - m).astype(BF16))

    if n_vis > 1:
        chosen = jnp.where(i >= 1, mask_scr[pl.ds(jp, 1), :], NEG_INF)
        t = s_prev + bias_prev_scr[...] + chosen
        m_new = jnp.maximum(m, jnp.max(t, axis=0, keepdims=True))
        acc = jnp.exp2(m - m_new) * acc + _dot(vt_scr[jp], jnp.exp2(t - m_new).astype(BF16))
        m = m_new

    for j in range(n_far):
        shift = far_bias + jnp.where(j < i - 1, mask_scr[j:j + 1, :], NEG_INF)
        s = far_s.pop(j)
        ja = j + ATTN_QK_AHEAD
        if ja < n_far:
            far_s[ja] = _dot(k_ref[ja * blk:(ja + 1) * blk, :], qt)
        m_new = jnp.maximum(m, jnp.max(s, axis=0, keepdims=True) + shift)
        p = jnp.exp2(s + (shift - m_new))
        acc = jnp.exp2(m - m_new) * acc + _dot(vt_scr[j], p.astype(BF16))
        m = m_new

    o_ref[...] = (acc[:hd, :] / acc[hd:hd + 1, :]).T.astype(o_ref.dtype)


def _moba_attention(qk, v, rel_bias):
    s = qk.shape[0]
    blk, hd, nh = MOBA_BLOCK, ATTN_HEAD_DIM, ATTN_HEADS
    assert s % blk == 0 and s // blk < LANES
    assert blk >= REL_MAX_DIST
    r = jnp.arange(blk)
    dist_own = r[None, :] - r[:, None]
    bkt_own = _t5_bucket(dist_own)
    bkt_prev = _t5_bucket(dist_own + blk)
    const = lambda h, i: (0, 0)
    return pl.pallas_call(
        _moba_kernel,
        grid=(nh, s // blk),
        in_specs=[
            pl.BlockSpec(memory_space=pltpu.SMEM),
            pl.BlockSpec((blk, blk), const),
            pl.BlockSpec((blk, blk), const),
            pl.BlockSpec((blk, hd), lambda h, i: (i, h)),
            pl.BlockSpec((s, hd), lambda h, i: (0, nh + h)),
            pl.BlockSpec((s, hd), lambda h, i: (0, h)),
        ],
        out_specs=pl.BlockSpec((blk, hd), lambda h, i: (i, h)),
        out_shape=jax.ShapeDtypeStruct((s, nh * hd), BF16),
        scratch_shapes=[
            pltpu.VMEM((LANES, hd), F32),
            pltpu.VMEM((blk, blk), F32),
            pltpu.VMEM((blk, blk), F32),
            pltpu.VMEM((s // blk, hd + ATTN_SUM_ROWS, blk), BF16),
            pltpu.VMEM((LANES, blk), F32),
        ],
        compiler_params=_cparams(("arbitrary", "arbitrary")),
        name="moba_attention",
    )(rel_bias, bkt_own, bkt_prev, qk, qk, v)


SSD_TAIL = 8
SSD_CONV_COLS = 512


def _softplus(x):
    return jnp.maximum(x, 0.0) + jnp.log(1.0 + jnp.exp(-jnp.abs(x)))


def _split3(x):
    hi = x.astype(BF16)
    r1 = x - hi.astype(F32)
    mid = r1.astype(BF16)
    lo = (r1 - mid.astype(F32)).astype(BF16)
    return hi, mid, lo


def _ssd_kernel(xbc_ref, z_ref, dt_ref, dtt_ref, wconv_ref, bconv_ref, dtb_ref, alog_ref,
                dtbt_ref, alogt_ref, dskip_ref, norm_ref, y_ref,
                tail_scr, xact_scr, state_scr):
    c = pl.program_id(0)
    L, P, N, W = SSD_CHUNK, SSD_HEAD_DIM, SSD_STATE, SSD_WIDTH
    assert P & (P - 1) == 0

    @pl.when(c == 0)
    def _():
        tail_scr[...] = jnp.zeros(tail_scr.shape, F32)
        state_scr[...] = jnp.zeros(state_scr.shape, F32)

    rr = lax.broadcasted_iota(jnp.int32, (L, L), 0)
    cc = lax.broadcasted_iota(jnp.int32, (L, L), 1)
    causal = rr >= cc

    shifts = [jnp.where(rr - cc == k, 1.0, 0.0).astype(BF16) for k in range(1, SSD_CONV)]
    row8 = lax.broadcasted_iota(jnp.int32, (SSD_TAIL, SSD_CONV_COLS), 0)
    for c0 in range(0, SSD_XBC, SSD_CONV_COLS):
        cols = slice(c0, c0 + SSD_CONV_COLS)
        x = xbc_ref[:, cols]
        tail = tail_scr[:, cols]
        acc = bconv_ref[:, cols] + wconv_ref[SSD_CONV - 1:SSD_CONV, cols] * x.astype(F32)
        fix = jnp.zeros((SSD_TAIL, SSD_CONV_COLS), F32)
        for k in range(1, SSD_CONV):
            wk = wconv_ref[SSD_CONV - 1 - k:SSD_CONV - k, cols]
            acc = acc + wk * _dot(shifts[k - 1], x)
            fix = fix + wk * jnp.where(row8 < k, pltpu.roll(tail, k, 0), 0.0)
        acc = jnp.concatenate([acc[:SSD_TAIL] + fix, acc[SSD_TAIL:]], axis=0)
        xact_scr[:, cols] = _silu(acc)
        tail_scr[:, cols] = x[L - SSD_TAIL:, :].astype(F32)

    dtv = _softplus(dt_ref[...] + dtb_ref[...])
    ad = dtv * (-jnp.exp(alog_ref[...]))
    dtt = _softplus(dtt_ref[...] + dtbt_ref[...])
    adt = dtt * (-jnp.exp(alogt_ref[...]))
    lower = jnp.where(causal, 1.0, 0.0).astype(BF16)
    upper = jnp.where(rr <= cc, 1.0, 0.0).astype(BF16)
    a_cs = sum(_dot(lower, t) for t in _split3(ad))
    a_cst = sum(_dot(t, upper) for t in _split3(adt))
    last = a_cs[L - 1:L, :]

    factors = jnp.concatenate([dtv, jnp.exp(a_cs), jnp.exp(last - a_cs)], axis=0).astype(BF16)
    group_w = W // SSD_GROUPS
    heads_per_group = SSD_HEADS // SSD_GROUPS
    e_row = lax.broadcasted_iota(jnp.int32, (LANES, group_w), 0)
    e_head = lax.shift_right_logical(lax.broadcasted_iota(jnp.int32, (LANES, group_w), 1),
                                     P.bit_length() - 1)
    pair_lane = lax.broadcasted_iota(jnp.int32, (L, 2 * P), 1)

    for g in range(SSD_GROUPS):
        gc = slice(g * group_w, (g + 1) * group_w)
        expand = jnp.where(e_row == g * heads_per_group + e_head, 1.0, 0.0).astype(BF16)
        spread = _dot(factors, expand)
        xs = xact_scr[:, gc]
        xd = xs * spread[:L]
        xd_b = xd.astype(BF16)
        xe_b = (xd * spread[2 * L:]).astype(BF16)
        bg = xact_scr[:, W + g * N:W + (g + 1) * N].astype(BF16)
        cg = xact_scr[:, W + SSD_BC + g * N:W + SSD_BC + (g + 1) * N].astype(BF16)
        st = state_scr[gc, :]
        y_off = _dot_nt(cg, st.astype(BF16)) * spread[L:2 * L]
        new = lax.dot_general(xe_b, bg, (((0,), (0,)), ((), ())), preferred_element_type=F32)
        cb = _dot_nt(cg, bg)
        pairs = []
        for r in range(0, heads_per_group, 2):
            h = g * heads_per_group + r
            xd_pair = xd_b[:, r * P:(r + 2) * P]
            halves = []
            for hh in (h, h + 1):
                dec = jnp.exp(jnp.where(causal, a_cs[:, hh:hh + 1] - a_cst[hh:hh + 1, :], -jnp.inf))
                halves.append(_dot((cb * dec).astype(BF16), xd_pair))
                hr = slice((hh - g * heads_per_group) * P, (hh - g * heads_per_group + 1) * P)
                state_scr[g * group_w + hr.start:g * group_w + hr.stop, :] = (
                    jnp.exp(a_cst[hh:hh + 1, L - 1:L]) * st[hr, :] + new[hr, :])
            pairs.append(jnp.where(pair_lane < P, halves[0], halves[1]))
        y = jnp.concatenate(pairs, axis=1) + y_off + xs * dskip_ref[:, gc]
        y = y * _silu(z_ref[:, gc].astype(F32))
        y = y * lax.rsqrt(jnp.mean(y * y, axis=-1, keepdims=True) + NORM_EPS)
        y_ref[:, gc] = (y * norm_ref[:, gc]).astype(y_ref.dtype)


def _ssd(xbc, z, dt, w_conv, b_conv, dt_bias, a_log, d_skip, norm):
    s = xbc.shape[0]
    L, nh = SSD_CHUNK, SSD_HEADS
    assert s % L == 0
    dtt = dt[:, :nh].T
    pad = lambda v: jnp.pad(v, (0, LANES - nh)).reshape(1, LANES)
    full = lambda r, cdim: pl.BlockSpec((r, cdim), lambda c: (0, 0))
    return pl.pallas_call(
        _ssd_kernel,
        grid=(s // L,),
        in_specs=[
            pl.BlockSpec((L, SSD_XBC), lambda c: (c, 0)),
            pl.BlockSpec((L, SSD_WIDTH), lambda c: (c, 0)),
            pl.BlockSpec((L, LANES), lambda c: (c, 0)),
            pl.BlockSpec((nh, L), lambda c: (0, c)),
            full(SSD_CONV, SSD_XBC), full(1, SSD_XBC),
            full(1, LANES), full(1, LANES), full(nh, 1), full(nh, 1),
            full(1, SSD_WIDTH), full(1, SSD_WIDTH),
        ],
        out_specs=pl.BlockSpec((L, SSD_WIDTH), lambda c: (c, 0)),
        out_shape=jax.ShapeDtypeStruct((s, SSD_WIDTH), BF16),
        scratch_shapes=[
            pltpu.VMEM((SSD_TAIL, SSD_XBC), F32),
            pltpu.VMEM((L, SSD_XBC), F32),
            pltpu.VMEM((SSD_WIDTH, SSD_STATE), F32),
        ],
        compiler_params=_cparams(("arbitrary",)),
        name="ssd_scan",
    )(xbc, z, dt, dtt, w_conv, b_conv.reshape(1, SSD_XBC), pad(dt_bias), pad(a_log),
      dt_bias.reshape(nh, 1), a_log.reshape(nh, 1),
      jnp.repeat(d_skip, SSD_HEAD_DIM).reshape(1, SSD_WIDTH), norm.reshape(1, SSD_WIDTH))


HALO = 16


def _merge_kernel(ya_ref, ys_ref, cb_ref, cc_ref, cx_ref, hc_ref, hx_ref, ga_ref, gs_ref, gc_ref,
                  wsc_ref, wa_ref, ws_ref, wc_ref, o_ref, ext_scr, yc_scr):
    i = pl.program_id(0)
    j = pl.program_id(1)
    tm = ya_ref.shape[0]

    @pl.when(j == 0)
    def _():
        halo = hc_ref[...].astype(F32) * hx_ref[...].astype(F32)
        ext_scr[0:HALO, :] = jnp.where(i > 0, halo, 0.0)
        for r0 in range(0, tm, ROW_CHUNK):
            rows = slice(r0, r0 + ROW_CHUNK)
            ext_scr[HALO + r0:HALO + r0 + ROW_CHUNK, :] = cc_ref[rows, :].astype(F32) * cx_ref[rows, :].astype(F32)
        for r0 in range(0, tm, ROW_CHUNK):
            rows = slice(r0, r0 + ROW_CHUNK)
            acc = jnp.zeros((ROW_CHUNK, CONV_WIDTH), F32)
            for k in range(CONV_K):
                start = HALO + r0 - (CONV_K - 1) + k
                acc = acc + wsc_ref[k:k + 1, :] * ext_scr[start:start + ROW_CHUNK, :]
            yc_scr[rows, :] = (cb_ref[rows, :].astype(F32) * acc).astype(BF16)

    merged = (_sigmoid(ga_ref[...].astype(F32)) * _dot(ya_ref[...], wa_ref[...].astype(BF16))
              + _sigmoid(gs_ref[...].astype(F32)) * _dot(ys_ref[...], ws_ref[...].astype(BF16))
              + _sigmoid(gc_ref[...].astype(F32)) * _dot(yc_scr[...], wc_ref[...].astype(BF16)))
    o_ref[...] = merged.astype(o_ref.dtype)


def _branch_merge(y_attn, y_ssd, tail, w_sc, w_a, w_s, w_c, lead, tm=1024, tn=256):
    s = y_attn.shape[0]
    d = w_a.shape[-1]
    tm = min(tm, s)
    cw = CONV_WIDTH
    g0 = 3 * cw // tn
    gd = d // tn
    row = lambda width, cb: pl.BlockSpec((tm, width), lambda i, j: (i, cb))
    halo = lambda cb: pl.BlockSpec((HALO, cw), lambda i, j: (jnp.maximum(i * (tm // HALO) - 1, 0), cb))
    gate = lambda k: pl.BlockSpec((tm, tn), lambda i, j: (i, g0 + k * gd + j))
    wcol = lambda kdim: _wspec(lead, (kdim, tn), lambda i, j: (0, j))
    return pl.pallas_call(
        _merge_kernel,
        grid=(s // tm, d // tn),
        in_specs=[
            row(ATTN_WIDTH, 0), row(SSD_WIDTH, 0),
            row(cw, 0), row(cw, 1), row(cw, 2), halo(1), halo(2),
            gate(0), gate(1), gate(2),
            pl.BlockSpec((CONV_K, cw), lambda i, j: (0, 0)),
            wcol(ATTN_WIDTH), wcol(SSD_WIDTH), wcol(cw),
        ],
        out_specs=pl.BlockSpec((tm, tn), lambda i, j: (i, j)),
        out_shape=jax.ShapeDtypeStruct((s, d), BF16),
        scratch_shapes=[pltpu.VMEM((HALO + tm, cw), F32), pltpu.VMEM((tm, cw), BF16)],
        compiler_params=_cparams(("arbitrary", "arbitrary")),
        name="branch_merge",
    )(y_attn, y_ssd, tail, tail, tail, tail, tail, tail, tail, tail, w_sc, w_a, w_s, w_c)


def _outproj_kernel(m_ref, w_ref, x_ref, gate_ref, o_ref):
    o_ref[...] = x_ref[...] + gate_ref[...] * _dot(m_ref[...], w_ref[...].astype(BF16))


def _out_proj(merged, w, lead, x, gate, tm=1024, tn=512):
    s, d = x.shape
    tm = min(tm, s)
    return pl.pallas_call(
        _outproj_kernel,
        grid=(s // tm, d // tn),
        in_specs=[
            pl.BlockSpec((tm, merged.shape[1]), lambda i, j: (i, 0)),
            _wspec(lead, (merged.shape[1], tn), lambda i, j: (0, j)),
            pl.BlockSpec((tm, tn), lambda i, j: (i, j)),
            pl.BlockSpec((1, tn), lambda i, j: (0, j)),
        ],
        out_specs=pl.BlockSpec((tm, tn), lambda i, j: (i, j)),
        out_shape=jax.ShapeDtypeStruct((s, d), F32),
        compiler_params=_cparams(("arbitrary", "arbitrary")),
        name="out_proj",
    )(merged, w, x, gate)


def _token_mix(x, gain, scale, shift, gate, l, w_mix_in, qk_norm, rel_bias, w_ssd_conv, b_ssd_conv,
               ssd_dt_bias, ssd_a_log, ssd_d, ssd_norm, w_sc_conv, w_br_attn, w_br_ssd, w_br_conv,
               w_mix_out):
    d = x.shape[1]
    lead = (l,)
    c_v = 2 * ATTN_WIDTH
    c_z = c_v + ATTN_WIDTH
    c_xbc = c_z + SSD_WIDTH
    c_dt = c_xbc + SSD_XBC
    c_tail = c_dt + SSD_HEADS
    n_tail = 3 * CONV_WIDTH + 3 * d
    assert w_mix_in.shape[-1] == c_tail + n_tail

    hgain = jnp.concatenate([jnp.tile(qk_norm[l, 0] * ATTN_Q_PRESCALE, ATTN_HEADS),
                             jnp.tile(qk_norm[l, 1], ATTN_HEADS)])[None]
    wt = jnp.swapaxes(w_mix_in, 1, 2)
    qk, h = _norm_qk_proj(x, gain, scale, shift, wt, lead, hgain)
    v = _proj(h, wt, lead, c_v, ATTN_WIDTH, BF16)
    z = _proj(h, wt, lead, c_z, SSD_WIDTH, BF16)
    xbc = _proj(h, wt, lead, c_xbc, SSD_XBC, BF16)
    wt_dt = jnp.pad(wt[l, c_dt:c_tail, :], ((0, LANES - SSD_HEADS), (0, 0)))
    dt = _proj(h, wt_dt, (), 0, LANES, F32)
    tail = _proj(h, wt, lead, c_tail, n_tail, BF16)

    y_attn = _moba_attention(qk, v, rel_bias)
    y_ssd = _ssd(xbc, z, dt, w_ssd_conv[l], b_ssd_conv[l], ssd_dt_bias[l], ssd_a_log[l], ssd_d[l], ssd_norm[l])
    merged = _branch_merge(y_attn, y_ssd, tail, w_sc_conv[l], w_br_attn, w_br_ssd, w_br_conv, lead)
    return _out_proj(merged, w_mix_out, lead, x, gate)


def kernel(x, c, w_ada, b_ada, norm_gain, w_ffn_in, w_ffn_out, w_mix_in, qk_norm, rel_bias, w_ssd_conv, b_ssd_conv, ssd_dt_bias, ssd_a_log, ssd_d, ssd_norm, w_sc_conv, w_br_attn, w_br_ssd, w_br_conv, w_mix_out):
    b, s, d = x.shape
    depth = w_ada.shape[0]
    assert b == 1 and s % math.lcm(MOBA_BLOCK, SSD_CHUNK) == 0
    xs = x.reshape(s, d)
    ada = _ada_proj(c, w_ada, b_ada).reshape(depth, N_SUBLAYERS, 3, 1, d)
    for l in range(depth):
        mod = lambda i: (norm_gain[l, i][None], ada[l, i, 1], ada[l, i, 0], ada[l, i, 2])
        xs = _ffn(xs, *mod(0), w_ffn_in, w_ffn_out, (l, 0))
        xs = _token_mix(xs, *mod(1), l, w_mix_in, qk_norm, rel_bias, w_ssd_conv, b_ssd_conv,
                        ssd_dt_bias, ssd_a_log, ssd_d, ssd_norm, w_sc_conv,
                        w_br_attn, w_br_ssd, w_br_conv, w_mix_out)
        xs = _ffn(xs, *mod(2), w_ffn_in, w_ffn_out, (l, 1))
    return xs.reshape(b, s, d)
```

```python
import functools
import math

import jax
import jax.numpy as jnp
from jax import lax
from jax.experimental import pallas as pl
from jax.experimental.pallas import tpu as pltpu

F32 = jnp.float32
BF16 = jnp.bfloat16

ATTN_HEADS = 8
ATTN_HEAD_DIM = 128
ATTN_WIDTH = ATTN_HEADS * ATTN_HEAD_DIM
MOBA_BLOCK = 256
MOBA_TOPK = 3
REL_BUCKETS = 32
REL_MAX_DIST = 128
SSD_HEADS = 32
SSD_HEAD_DIM = 64
SSD_WIDTH = SSD_HEADS * SSD_HEAD_DIM
SSD_GROUPS = 4
SSD_STATE = 128
SSD_CONV = 4
SSD_CHUNK = 256
SSD_BC = SSD_GROUPS * SSD_STATE
SSD_XBC = SSD_WIDTH + 2 * SSD_BC
CONV_WIDTH = 1024
CONV_K = 3
N_SUBLAYERS = 3
FFN_RESIDUAL = 0.5
NORM_EPS = 1e-6
NEG_INF = -1e30

LANES = 128
VMEM_LIMIT_BYTES = 58 * 1024 * 1024


def _cparams(semantics):
    return pltpu.CompilerParams(dimension_semantics=semantics, vmem_limit_bytes=VMEM_LIMIT_BYTES)


def _sigmoid(x):
    return 1.0 / (1.0 + jnp.exp(-x))


def _silu(x):
    return x * _sigmoid(x)


def _dot(a, b):
    return jnp.dot(a, b, preferred_element_type=F32)


def _dot_nt(a, b):
    return lax.dot_general(a, b, (((1,), (1,)), ((), ())), preferred_element_type=F32)


def _wspec(lead, block, index_map):
    return pl.BlockSpec((None,) * len(lead) + block, lambda i, j: lead + index_map(i, j))


def _mod_norm(x, gain, scale, shift):
    y = x * lax.rsqrt(jnp.mean(x * x, axis=-1, keepdims=True) + NORM_EPS)
    return (y * gain) * (1.0 + scale) + shift


def _ada_kernel(c_ref, w_ref, b_ref, o_ref):
    cond = _silu(c_ref[...]).astype(BF16)
    o_ref[...] = _dot(cond, w_ref[...].astype(BF16)) + b_ref[...]


def _ada_proj(c, w_ada, b_ada, tn=1024):
    depth, d, n = w_ada.shape
    c8 = jnp.broadcast_to(c, (8, d))
    out = pl.pallas_call(
        _ada_kernel,
        grid=(depth, n // tn),
        in_specs=[
            pl.BlockSpec((8, d), lambda l, j: (0, 0)),
            pl.BlockSpec((None, d, tn), lambda l, j: (l, 0, j)),
            pl.BlockSpec((None, 1, tn), lambda l, j: (l, 0, j)),
        ],
        out_specs=pl.BlockSpec((None, 8, tn), lambda l, j: (l, 0, j)),
        out_shape=jax.ShapeDtypeStruct((depth, 8, n), F32),
        compiler_params=_cparams(("arbitrary", "arbitrary")),
        name="ada_proj",
    )(c8, w_ada, b_ada.reshape(depth, 1, n))
    return out[:, 0, :]


ROW_CHUNK = 128
COL_CHUNK = 512


def _ffn_kernel(x_ref, gain_ref, scale_ref, shift_ref, gate_ref, wg_ref, wu_ref, wo_ref,
                o_ref, h_scr):
    j = pl.program_id(1)
    tm, d = x_ref.shape

    @pl.when(j == 0)
    def _():
        def body(r, carry):
            rows = pl.ds(pl.multiple_of(r * ROW_CHUNK, ROW_CHUNK), ROW_CHUNK)
            h = _mod_norm(x_ref[rows, :], gain_ref[...], scale_ref[...], shift_ref[...])
            h_scr[rows, :] = h.astype(BF16)
            o_ref[rows, :] = jnp.zeros((ROW_CHUNK, d), F32)
            return carry
        lax.fori_loop(0, tm // ROW_CHUNK, body, 0)

    h = h_scr[...]
    g = _dot(h, wg_ref[...].astype(BF16))
    u = _dot(h, wu_ref[...].astype(BF16))
    a = (_silu(g) * u).astype(BF16)
    for c in range(0, d, COL_CHUNK):
        o_ref[:, c:c + COL_CHUNK] += _dot(a, wo_ref[:, c:c + COL_CHUNK].astype(BF16))

    @pl.when(j == pl.num_programs(1) - 1)
    def _():
        def body(r, carry):
            rows = pl.ds(pl.multiple_of(r * ROW_CHUNK, ROW_CHUNK), ROW_CHUNK)
            o_ref[rows, :] = x_ref[rows, :] + (FFN_RESIDUAL * gate_ref[...]) * o_ref[rows, :]
            return carry
        lax.fori_loop(0, tm // ROW_CHUNK, body, 0)


def _ffn(x, gain, scale, shift, gate, w_in, w_out, lead, tm=1024, tf=256):
    s, d = x.shape
    f = w_out.shape[-2]
    tm = min(tm, s)
    nf = f // tf
    vec = pl.BlockSpec((1, d), lambda i, j: (0, 0))
    return pl.pallas_call(
        _ffn_kernel,
        grid=(s // tm, nf),
        in_specs=[
            pl.BlockSpec((tm, d), lambda i, j: (i, 0)),
            vec, vec, vec, vec,
            _wspec(lead, (d, tf), lambda i, j: (0, j)),
            _wspec(lead, (d, tf), lambda i, j: (0, j + nf)),
            _wspec(lead, (tf, d), lambda i, j: (j, 0)),
        ],
        out_specs=pl.BlockSpec((tm, d), lambda i, j: (i, 0)),
        out_shape=jax.ShapeDtypeStruct((s, d), F32),
        scratch_shapes=[pltpu.VMEM((tm, d), BF16)],
        compiler_params=_cparams(("arbitrary", "arbitrary")),
        name="ffn",
    )(x, gain, scale, shift, gate, w_in, w_in, w_out)


def _head_rms_norm(y, gain):
    outs = []
    for c in range(0, y.shape[1], ATTN_HEAD_DIM):
        yc = y[:, c:c + ATTN_HEAD_DIM]
        outs.append(yc * lax.rsqrt(jnp.mean(yc * yc, axis=-1, keepdims=True) + NORM_EPS))
    return jnp.concatenate(outs, axis=1) * gain


def _normproj_kernel(x_ref, gain_ref, scale_ref, shift_ref, w_ref, hgain_ref, o_ref, h_ref):
    j = pl.program_id(1)
    tm = x_ref.shape[0]

    @pl.when(j == 0)
    def _():
        def body(r, carry):
            rows = pl.ds(pl.multiple_of(r * ROW_CHUNK, ROW_CHUNK), ROW_CHUNK)
            h = _mod_norm(x_ref[rows, :], gain_ref[...], scale_ref[...], shift_ref[...])
            h_ref[rows, :] = h.astype(BF16)
            return carry
        lax.fori_loop(0, tm // ROW_CHUNK, body, 0)

    y = _dot_nt(h_ref[...], w_ref[...].astype(BF16))
    o_ref[...] = _head_rms_norm(y, hgain_ref[...]).astype(o_ref.dtype)


def _norm_qk_proj(x, gain, scale, shift, wt, lead, hgain, tm=1024, tn=512):
    s, d = x.shape
    n = 2 * ATTN_WIDTH
    tm = min(tm, s)
    vec = pl.BlockSpec((1, d), lambda i, j: (0, 0))
    return pl.pallas_call(
        _normproj_kernel,
        grid=(s // tm, n // tn),
        in_specs=[
            pl.BlockSpec((tm, d), lambda i, j: (i, 0)),
            vec, vec, vec,
            _wspec(lead, (tn, d), lambda i, j: (j, 0)),
            pl.BlockSpec((1, tn), lambda i, j: (0, j)),
        ],
        out_specs=[
            pl.BlockSpec((tm, tn), lambda i, j: (i, j)),
            pl.BlockSpec((tm, d), lambda i, j: (i, 0)),
        ],
        out_shape=[jax.ShapeDtypeStruct((s, n), BF16), jax.ShapeDtypeStruct((s, d), BF16)],
        compiler_params=_cparams(("arbitrary", "arbitrary")),
        name="norm_qk_proj",
    )(x, gain, scale, shift, wt, hgain)


def _proj_kernel(h_ref, *refs, shift):
    wa_ref, o_ref, w_scr = refs[0], refs[-2], refs[-1]
    tn = wa_ref.shape[0]

    @pl.when(pl.program_id(1) == 0)
    def _():
        w_scr[:tn - shift, :] = wa_ref[shift:, :].astype(BF16)
        if shift:
            w_scr[tn - shift:, :] = refs[1][:shift, :].astype(BF16)

    o_ref[...] = _dot_nt(h_ref[...], w_scr[...]).astype(o_ref.dtype)


PROJ_SHIFT_ROWS = 128


def _proj(h, wt, lead, row0, n, out_dtype, tm=1024, tn=1024):
    s, d = h.shape
    tm = min(tm, s)
    tn = min(tn, n)
    base = row0 // tn * tn
    shift = row0 - base
    assert n % tn == 0 and shift % 16 == 0 and shift <= PROJ_SHIFT_ROWS and tn % PROJ_SHIFT_ROWS == 0
    w_specs = [_wspec(lead, (tn, d), lambda j, i: (base // tn + j, 0))]
    if shift:
        w_specs.append(_wspec(lead, (PROJ_SHIFT_ROWS, d),
                              lambda j, i: ((base + (j + 1) * tn) // PROJ_SHIFT_ROWS, 0)))
    return pl.pallas_call(
        functools.partial(_proj_kernel, shift=shift),
        grid=(n // tn, s // tm),
        in_specs=[pl.BlockSpec((tm, d), lambda j, i: (i, 0))] + w_specs,
        out_specs=pl.BlockSpec((tm, tn), lambda j, i: (i, j)),
        out_shape=jax.ShapeDtypeStruct((s, n), out_dtype),
        scratch_shapes=[pltpu.VMEM((tn, d), BF16)],
        compiler_params=_cparams(("arbitrary", "arbitrary")),
        name="proj",
    )(h, *([wt] * len(w_specs)))


def _t5_bucket(dist):
    n = jnp.maximum(dist, 0)
    max_exact = REL_BUCKETS // 2
    ratio = jnp.log(jnp.maximum(n, 1).astype(F32) / max_exact) / math.log(REL_MAX_DIST / max_exact)
    large = max_exact + (ratio * (REL_BUCKETS - max_exact)).astype(jnp.int32)
    large = jnp.minimum(large, REL_BUCKETS - 1)
    return jnp.where(n < max_exact, n, large)


ATTN_VIS_STEPS = (4, 8, 12, 16, 20, 24, 28, 32)
ATTN_QK_AHEAD = 5
ATTN_SUM_ROWS = 16
LOG2E = 1.4426950408889634
ATTN_Q_PRESCALE = ATTN_HEAD_DIM ** -0.5 * LOG2E


def _moba_kernel(rel_ref, bkt_own_ref, bkt_prev_ref, q_ref, k_ref, v_ref, o_ref,
                 kmean_scr, bias_own_scr, bias_prev_scr, vt_scr, mask_scr):
    h = pl.program_id(0)
    i = pl.program_id(1)
    blk, hd = MOBA_BLOCK, ATTN_HEAD_DIM
    nb = k_ref.shape[0] // blk

    @pl.when(i == 0)
    def _():
        kmean_scr[...] = jnp.zeros(kmean_scr.shape, F32)

        ones_row = jnp.where(lax.broadcasted_iota(jnp.int32, (ATTN_SUM_ROWS, blk), 0) == 0, 1.0, 0.0)

        def block_body(b, carry):
            rows = pl.ds(pl.multiple_of(b * blk, blk), blk)
            kmean_scr[pl.ds(b, 1), :] = jnp.mean(k_ref[rows, :].astype(F32), axis=0, keepdims=True)
            vt_scr[b, :hd, :] = v_ref[rows, :].astype(F32).T.astype(BF16)
            vt_scr[b, hd:, :] = ones_row.astype(BF16)
            return carry
        lax.fori_loop(0, nb, block_body, 0)

        key = lax.broadcasted_iota(jnp.int32, (blk, blk), 0)
        qry = lax.broadcasted_iota(jnp.int32, (blk, blk), 1)
        bias_own_scr[...] = jnp.where(key <= qry, 0.0, NEG_INF)
        bias_prev_scr[...] = jnp.zeros((blk, blk), F32)

        def bias_body(b, carry):
            val = rel_ref[b, h] * LOG2E
            bias_own_scr[...] += jnp.where(bkt_own_ref[...] == b, val, 0.0)
            bias_prev_scr[...] += jnp.where(bkt_prev_ref[...] == b, val, 0.0)
            return carry
        lax.fori_loop(0, REL_BUCKETS, bias_body, 0)

    q = q_ref[...]

    km = kmean_scr[...]
    km_hi = km.astype(BF16)
    km_lo = (km - km_hi.astype(F32)).astype(BF16)
    gate = _dot_nt(km_hi, q) + _dot_nt(km_lo, q)
    rowb = lax.broadcasted_iota(jnp.int32, gate.shape, 0)
    rowf = rowb.astype(F32)
    gate = jnp.where(rowb < i, gate, -jnp.inf)
    sel = jnp.zeros(gate.shape, F32)
    for _ in range(min(MOBA_TOPK, nb)):
        top = jnp.max(gate, axis=0, keepdims=True)
        idx = jnp.min(jnp.where(gate == top, rowf, float(LANES)), axis=0, keepdims=True)
        hit = rowf == idx
        sel = jnp.where(hit & (top > -jnp.inf), 1.0, sel)
        gate = jnp.where(hit, -jnp.inf, gate)
    mask_scr[...] = jnp.where(sel > 0.0, 0.0, NEG_INF)

    far_bias = rel_ref[REL_BUCKETS - 1, h] * LOG2E
    visible = i + 1
    lo = 0
    for n_vis in sorted({min(v, nb) for v in ATTN_VIS_STEPS} | {nb}):
        @pl.when((visible > lo) & (visible <= n_vis))
        def _(n_vis=n_vis):
            _moba_tile(n_vis, i, far_bias, q, k_ref, o_ref, bias_own_scr, bias_prev_scr, vt_scr, mask_scr)
        lo = n_vis


def _moba_tile(n_vis, i, far_bias, q, k_ref, o_ref, bias_own_scr, bias_prev_scr, vt_scr, mask_scr):
    blk, hd = MOBA_BLOCK, ATTN_HEAD_DIM
    qt = q.astype(F32).T.astype(BF16)

    def rows(j):
        return pl.ds(pl.multiple_of(j * blk, blk), blk)

    n_far = n_vis - 2
    jp = jnp.maximum(i - 1, 0)
    s_own = _dot(k_ref[rows(i), :], qt)
    s_prev = _dot(k_ref[rows(jp), :], qt)

    def far_qk(j):
        return _dot(k_ref[j * blk:(j + 1) * blk, :], qt)
    far_s = {j: far_qk(j) for j in range(min(ATTN_QK_AHEAD, n_far))}

    chosen = jnp.where(i >= 1, mask_scr[pl.ds(jp, 1), :], NEG_INF)
    t_own = s_own + bias_own_scr[...]
    t_prev = s_prev + bias_prev_scr[...] + chosen
    m = jnp.maximum(jnp.max(t_own, axis=0, keepdims=True), jnp.max(t_prev, axis=0, keepdims=True))
    p = jnp.concatenate([jnp.exp2(t_prev - m), jnp.exp2(t_own - m)], axis=0).astype(BF16)
    acc = _dot(jnp.concatenate([vt_scr[jp], vt_scr[i]], axis=1), p)

    for j in range(n_far):
        shift = far_bias + jnp.where(j < i - 1, mask_scr[j:j + 1, :], NEG_INF)
        s = far_s.pop(j)
        if j + ATTN_QK_AHEAD < n_far:
            far_s[j + ATTN_QK_AHEAD] = far_qk(j + ATTN_QK_AHEAD)
        m_new = jnp.maximum(m, jnp.max(s, axis=0, keepdims=True) + shift)
        p = jnp.exp2(s + (shift - m_new))
        acc = jnp.exp2(m - m_new) * acc + _dot(vt_scr[j], p.astype(BF16))
        m = m_new

    o_ref[...] = (acc[:hd, :] / acc[hd:hd + 1, :]).T.astype(o_ref.dtype)


def _moba_attention(qk, v, rel_bias):
    s = qk.shape[0]
    blk, hd, nh = MOBA_BLOCK, ATTN_HEAD_DIM, ATTN_HEADS
    assert s % blk == 0 and s // blk < LANES
    assert blk >= REL_MAX_DIST
    r = jnp.arange(blk)
    dist_own = r[None, :] - r[:, None]
    bkt_own = _t5_bucket(dist_own)
    bkt_prev = _t5_bucket(dist_own + blk)
    const = lambda h, i: (0, 0)
    return pl.pallas_call(
        _moba_kernel,
        grid=(nh, s // blk),
        in_specs=[
            pl.BlockSpec(memory_space=pltpu.SMEM),
            pl.BlockSpec((blk, blk), const),
            pl.BlockSpec((blk, blk), const),
            pl.BlockSpec((blk, hd), lambda h, i: (i, h)),
            pl.BlockSpec((s, hd), lambda h, i: (0, nh + h)),
            pl.BlockSpec((s, hd), lambda h, i: (0, h)),
        ],
        out_specs=pl.BlockSpec((blk, hd), lambda h, i: (i, h)),
        out_shape=jax.ShapeDtypeStruct((s, nh * hd), BF16),
        scratch_shapes=[
            pltpu.VMEM((LANES, hd), F32),
            pltpu.VMEM((blk, blk), F32),
            pltpu.VMEM((blk, blk), F32),
            pltpu.VMEM((s // blk, hd + ATTN_SUM_ROWS, blk), BF16),
            pltpu.VMEM((LANES, blk), F32),
        ],
        compiler_params=_cparams(("arbitrary", "arbitrary")),
        name="moba_attention",
    )(rel_bias, bkt_own, bkt_prev, qk, qk, v)


SSD_TAIL = 8
SSD_CONV_COLS = 512


def _softplus(x):
    return jnp.maximum(x, 0.0) + jnp.log(1.0 + jnp.exp(-jnp.abs(x)))


def _split3(x):
    hi = x.astype(BF16)
    r1 = x - hi.astype(F32)
    mid = r1.astype(BF16)
    lo = (r1 - mid.astype(F32)).astype(BF16)
    return hi, mid, lo


def _ssd_kernel(xbc_ref, z_ref, dt_ref, dtt_ref, wconv_ref, bconv_ref, dtb_ref, alog_ref,
                dtbt_ref, alogt_ref, dskip_ref, norm_ref, y_ref,
                tail_scr, xact_scr, state_scr):
    c = pl.program_id(0)
    L, P, N, W = SSD_CHUNK, SSD_HEAD_DIM, SSD_STATE, SSD_WIDTH
    assert P & (P - 1) == 0

    @pl.when(c == 0)
    def _():
        tail_scr[...] = jnp.zeros(tail_scr.shape, F32)
        state_scr[...] = jnp.zeros(state_scr.shape, F32)

    rr = lax.broadcasted_iota(jnp.int32, (L, L), 0)
    cc = lax.broadcasted_iota(jnp.int32, (L, L), 1)
    causal = rr >= cc

    shifts = [jnp.where(rr - cc == k, 1.0, 0.0).astype(BF16) for k in range(1, SSD_CONV)]
    row8 = lax.broadcasted_iota(jnp.int32, (SSD_TAIL, SSD_CONV_COLS), 0)
    for c0 in range(0, SSD_XBC, SSD_CONV_COLS):
        cols = slice(c0, c0 + SSD_CONV_COLS)
        x = xbc_ref[:, cols]
        tail = tail_scr[:, cols]
        acc = bconv_ref[:, cols] + wconv_ref[SSD_CONV - 1:SSD_CONV, cols] * x.astype(F32)
        fix = jnp.zeros((SSD_TAIL, SSD_CONV_COLS), F32)
        for k in range(1, SSD_CONV):
            wk = wconv_ref[SSD_CONV - 1 - k:SSD_CONV - k, cols]
            acc = acc + wk * _dot(shifts[k - 1], x)
            fix = fix + wk * jnp.where(row8 < k, pltpu.roll(tail, k, 0), 0.0)
        acc = jnp.concatenate([acc[:SSD_TAIL] + fix, acc[SSD_TAIL:]], axis=0)
        xact_scr[:, cols] = _silu(acc)
        tail_scr[:, cols] = x[L - SSD_TAIL:, :].astype(F32)

    dtv = _softplus(dt_ref[...] + dtb_ref[...])
    ad = dtv * (-jnp.exp(alog_ref[...]))
    dtt = _softplus(dtt_ref[...] + dtbt_ref[...])
    adt = dtt * (-jnp.exp(alogt_ref[...]))
    lower = jnp.where(causal, 1.0, 0.0).astype(BF16)
    upper = jnp.where(rr <= cc, 1.0, 0.0).astype(BF16)
    a_cs = sum(_dot(lower, t) for t in _split3(ad))
    a_cst = sum(_dot(t, upper) for t in _split3(adt))
    last = a_cs[L - 1:L, :]

    factors = jnp.concatenate([dtv, jnp.exp(a_cs), jnp.exp(last - a_cs)], axis=0).astype(BF16)
    group_w = W // SSD_GROUPS
    heads_per_group = SSD_HEADS // SSD_GROUPS
    e_row = lax.broadcasted_iota(jnp.int32, (LANES, group_w), 0)
    e_head = lax.shift_right_logical(lax.broadcasted_iota(jnp.int32, (LANES, group_w), 1),
                                     P.bit_length() - 1)
    pair_lane = lax.broadcasted_iota(jnp.int32, (L, 2 * P), 1)

    for g in range(SSD_GROUPS):
        gc = slice(g * group_w, (g + 1) * group_w)
        expand = jnp.where(e_row == g * heads_per_group + e_head, 1.0, 0.0).astype(BF16)
        spread = _dot(factors, expand)
        xs = xact_scr[:, gc]
        xd = xs * spread[:L]
        xd_b = xd.astype(BF16)
        xe_b = (xd * spread[2 * L:]).astype(BF16)
        bg = xact_scr[:, W + g * N:W + (g + 1) * N].astype(BF16)
        cg = xact_scr[:, W + SSD_BC + g * N:W + SSD_BC + (g + 1) * N].astype(BF16)
        st = state_scr[gc, :]
        y_off = _dot_nt(cg, st.astype(BF16)) * spread[L:2 * L]
        new = lax.dot_general(xe_b, bg, (((0,), (0,)), ((), ())), preferred_element_type=F32)
        cb = _dot_nt(cg, bg)
        pairs = []
        for r in range(0, heads_per_group, 2):
            h = g * heads_per_group + r
            xd_pair = xd_b[:, r * P:(r + 2) * P]
            halves = []
            for hh in (h, h + 1):
                dec = jnp.exp(jnp.where(causal, a_cs[:, hh:hh + 1] - a_cst[hh:hh + 1, :], -jnp.inf))
                halves.append(_dot((cb * dec).astype(BF16), xd_pair))
                hr = slice((hh - g * heads_per_group) * P, (hh - g * heads_per_group + 1) * P)
                state_scr[g * group_w + hr.start:g * group_w + hr.stop, :] = (
                    jnp.exp(a_cst[hh:hh + 1, L - 1:L]) * st[hr, :] + new[hr, :])
            pairs.append(jnp.where(pair_lane < P, halves[0], halves[1]))
        y = jnp.concatenate(pairs, axis=1) + y_off + xs * dskip_ref[:, gc]
        y = y * _silu(z_ref[:, gc].astype(F32))
        y = y * lax.rsqrt(jnp.mean(y * y, axis=-1, keepdims=True) + NORM_EPS)
        y_ref[:, gc] = (y * norm_ref[:, gc]).astype(y_ref.dtype)


def _ssd(xbc, z, dt, w_conv, b_conv, dt_bias, a_log, d_skip, norm):
    s = xbc.shape[0]
    L, nh = SSD_CHUNK, SSD_HEADS
    assert s % L == 0
    dtt = dt[:, :nh].T
    pad = lambda v: jnp.pad(v, (0, LANES - nh)).reshape(1, LANES)
    full = lambda r, cdim: pl.BlockSpec((r, cdim), lambda c: (0, 0))
    return pl.pallas_call(
        _ssd_kernel,
        grid=(s // L,),
        in_specs=[
            pl.BlockSpec((L, SSD_XBC), lambda c: (c, 0)),
            pl.BlockSpec((L, SSD_WIDTH), lambda c: (c, 0)),
            pl.BlockSpec((L, LANES), lambda c: (c, 0)),
            pl.BlockSpec((nh, L), lambda c: (0, c)),
            full(SSD_CONV, SSD_XBC), full(1, SSD_XBC),
            full(1, LANES), full(1, LANES), full(nh, 1), full(nh, 1),
            full(1, SSD_WIDTH), full(1, SSD_WIDTH),
        ],
        out_specs=pl.BlockSpec((L, SSD_WIDTH), lambda c: (c, 0)),
        out_shape=jax.ShapeDtypeStruct((s, SSD_WIDTH), BF16),
        scratch_shapes=[
            pltpu.VMEM((SSD_TAIL, SSD_XBC), F32),
            pltpu.VMEM((L, SSD_XBC), F32),
            pltpu.VMEM((SSD_WIDTH, SSD_STATE), F32),
        ],
        compiler_params=_cparams(("arbitrary",)),
        name="ssd_scan",
    )(xbc, z, dt, dtt, w_conv, b_conv.reshape(1, SSD_XBC), pad(dt_bias), pad(a_log),
      dt_bias.reshape(nh, 1), a_log.reshape(nh, 1),
      jnp.repeat(d_skip, SSD_HEAD_DIM).reshape(1, SSD_WIDTH), norm.reshape(1, SSD_WIDTH))


HALO = 16


def _merge_kernel(ya_ref, ys_ref, cb_ref, cc_ref, cx_ref, hc_ref, hx_ref, ga_ref, gs_ref, gc_ref,
                  wsc_ref, wa_ref, ws_ref, wc_ref, o_ref, ext_scr, yc_scr):
    i = pl.program_id(0)
    j = pl.program_id(1)
    tm = ya_ref.shape[0]

    @pl.when(j == 0)
    def _():
        halo = hc_ref[...].astype(F32) * hx_ref[...].astype(F32)
        ext_scr[0:HALO, :] = jnp.where(i > 0, halo, 0.0)
        for r0 in range(0, tm, ROW_CHUNK):
            rows = slice(r0, r0 + ROW_CHUNK)
            ext_scr[HALO + r0:HALO + r0 + ROW_CHUNK, :] = cc_ref[rows, :].astype(F32) * cx_ref[rows, :].astype(F32)
        for r0 in range(0, tm, ROW_CHUNK):
            rows = slice(r0, r0 + ROW_CHUNK)
            acc = jnp.zeros((ROW_CHUNK, CONV_WIDTH), F32)
            for k in range(CONV_K):
                start = HALO + r0 - (CONV_K - 1) + k
                acc = acc + wsc_ref[k:k + 1, :] * ext_scr[start:start + ROW_CHUNK, :]
            yc_scr[rows, :] = (cb_ref[rows, :].astype(F32) * acc).astype(BF16)

    merged = (_sigmoid(ga_ref[...].astype(F32)) * _dot(ya_ref[...], wa_ref[...].astype(BF16))
              + _sigmoid(gs_ref[...].astype(F32)) * _dot(ys_ref[...], ws_ref[...].astype(BF16))
              + _sigmoid(gc_ref[...].astype(F32)) * _dot(yc_scr[...], wc_ref[...].astype(BF16)))
    o_ref[...] = merged.astype(o_ref.dtype)


def _branch_merge(y_attn, y_ssd, tail, w_sc, w_a, w_s, w_c, lead, tm=1024, tn=512):
    s = y_attn.shape[0]
    d = w_a.shape[-1]
    tm = min(tm, s)
    cw = CONV_WIDTH
    g0 = 3 * cw // tn
    gd = d // tn
    row = lambda width, cb: pl.BlockSpec((tm, width), lambda i, j: (i, cb))
    halo = lambda cb: pl.BlockSpec((HALO, cw), lambda i, j: (jnp.maximum(i * (tm // HALO) - 1, 0), cb))
    gate = lambda k: pl.BlockSpec((tm, tn), lambda i, j: (i, g0 + k * gd + j))
    wcol = lambda kdim: _wspec(lead, (kdim, tn), lambda i, j: (0, j))
    return pl.pallas_call(
        _merge_kernel,
        grid=(s // tm, d // tn),
        in_specs=[
            row(ATTN_WIDTH, 0), row(SSD_WIDTH, 0),
            row(cw, 0), row(cw, 1), row(cw, 2), halo(1), halo(2),
            gate(0), gate(1), gate(2),
            pl.BlockSpec((CONV_K, cw), lambda i, j: (0, 0)),
            wcol(ATTN_WIDTH), wcol(SSD_WIDTH), wcol(cw),
        ],
        out_specs=pl.BlockSpec((tm, tn), lambda i, j: (i, j)),
        out_shape=jax.ShapeDtypeStruct((s, d), BF16),
        scratch_shapes=[pltpu.VMEM((HALO + tm, cw), F32), pltpu.VMEM((tm, cw), BF16)],
        compiler_params=_cparams(("arbitrary", "arbitrary")),
        name="branch_merge",
    )(y_attn, y_ssd, tail, tail, tail, tail, tail, tail, tail, tail, w_sc, w_a, w_s, w_c)


def _outproj_kernel(m_ref, w_ref, x_ref, gate_ref, o_ref):
    o_ref[...] = x_ref[...] + gate_ref[...] * _dot(m_ref[...], w_ref[...].astype(BF16))


def _out_proj(merged, w, lead, x, gate, tm=1024, tn=1024):
    s, d = x.shape
    tm = min(tm, s)
    return pl.pallas_call(
        _outproj_kernel,
        grid=(d // tn, s // tm),
        in_specs=[
            pl.BlockSpec((tm, merged.shape[1]), lambda j, i: (i, 0)),
            _wspec(lead, (merged.shape[1], tn), lambda j, i: (0, j)),
            pl.BlockSpec((tm, tn), lambda j, i: (i, j)),
            pl.BlockSpec((1, tn), lambda j, i: (0, j)),
        ],
        out_specs=pl.BlockSpec((tm, tn), lambda j, i: (i, j)),
        out_shape=jax.ShapeDtypeStruct((s, d), F32),
        compiler_params=_cparams(("arbitrary", "arbitrary")),
        name="out_proj",
    )(merged, w, x, gate)


def _token_mix(x, gain, scale, shift, gate, l, w_mix_in, qk_norm, rel_bias, w_ssd_conv, b_ssd_conv,
               ssd_dt_bias, ssd_a_log, ssd_d, ssd_norm, w_sc_conv, w_br_attn, w_br_ssd, w_br_conv,
               w_mix_out):
    d = x.shape[1]
    lead = (l,)
    c_v = 2 * ATTN_WIDTH
    c_z = c_v + ATTN_WIDTH
    c_xbc = c_z + SSD_WIDTH
    c_dt = c_xbc + SSD_XBC
    c_tail = c_dt + SSD_HEADS
    n_tail = 3 * CONV_WIDTH + 3 * d
    assert w_mix_in.shape[-1] == c_tail + n_tail

    hgain = jnp.concatenate([jnp.tile(qk_norm[l, 0] * ATTN_Q_PRESCALE, ATTN_HEADS),
                             jnp.tile(qk_norm[l, 1], ATTN_HEADS)])[None]
    wt = jnp.swapaxes(w_mix_in, 1, 2)
    qk, h = _norm_qk_proj(x, gain, scale, shift, wt, lead, hgain)
    v = _proj(h, wt, lead, c_v, ATTN_WIDTH, BF16)
    z = _proj(h, wt, lead, c_z, SSD_WIDTH, BF16)
    xbc = _proj(h, wt, lead, c_xbc, SSD_XBC, BF16)
    wt_dt = jnp.pad(wt[l, c_dt:c_tail, :], ((0, LANES - SSD_HEADS), (0, 0)))
    dt = _proj(h, wt_dt, (), 0, LANES, F32)
    tail = _proj(h, wt, lead, c_tail, n_tail, BF16)

    y_attn = _moba_attention(qk, v, rel_bias)
    y_ssd = _ssd(xbc, z, dt, w_ssd_conv[l], b_ssd_conv[l], ssd_dt_bias[l], ssd_a_log[l], ssd_d[l], ssd_norm[l])
    merged = _branch_merge(y_attn, y_ssd, tail, w_sc_conv[l], w_br_attn[l].astype(BF16),
                           w_br_ssd[l].astype(BF16), w_br_conv[l].astype(BF16), ())
    return _out_proj(merged, w_mix_out, lead, x, gate)


def kernel(x, c, w_ada, b_ada, norm_gain, w_ffn_in, w_ffn_out, w_mix_in, qk_norm, rel_bias, w_ssd_conv, b_ssd_conv, ssd_dt_bias, ssd_a_log, ssd_d, ssd_norm, w_sc_conv, w_br_attn, w_br_ssd, w_br_conv, w_mix_out):
    b, s, d = x.shape
    depth = w_ada.shape[0]
    assert b == 1 and s % math.lcm(MOBA_BLOCK, SSD_CHUNK) == 0
    xs = x.reshape(s, d)
    ada = _ada_proj(c, w_ada, b_ada).reshape(depth, N_SUBLAYERS, 3, 1, d)
    for l in range(depth):
        mod = lambda i: (norm_gain[l, i][None], ada[l, i, 1], ada[l, i, 0], ada[l, i, 2])
        xs = _ffn(xs, *mod(0), w_ffn_in, w_ffn_out, (l, 0))
        xs = _token_mix(xs, *mod(1), l, w_mix_in, qk_norm, rel_bias, w_ssd_conv, b_ssd_conv,
                        ssd_dt_bias, ssd_a_log, ssd_d, ssd_norm, w_sc_conv,
                        w_br_attn, w_br_ssd, w_br_conv, w_mix_out)
        xs = _ffn(xs, *mod(2), w_ffn_in, w_ffn_out, (l, 1))
    return xs.reshape(b, s, d)
```

```python
import functools
import math

import jax
import jax.numpy as jnp
from jax import lax
from jax.experimental import pallas as pl
from jax.experimental.pallas import tpu as pltpu

F32 = jnp.float32
BF16 = jnp.bfloat16

ATTN_HEADS = 8
ATTN_HEAD_DIM = 128
ATTN_WIDTH = ATTN_HEADS * ATTN_HEAD_DIM
MOBA_BLOCK = 256
MOBA_TOPK = 3
REL_BUCKETS = 32
REL_MAX_DIST = 128
SSD_HEADS = 32
SSD_HEAD_DIM = 64
SSD_WIDTH = SSD_HEADS * SSD_HEAD_DIM
SSD_GROUPS = 4
SSD_STATE = 128
SSD_CONV = 4
SSD_CHUNK = 256
SSD_BC = SSD_GROUPS * SSD_STATE
SSD_XBC = SSD_WIDTH + 2 * SSD_BC
CONV_WIDTH = 1024
CONV_K = 3
N_SUBLAYERS = 3
FFN_RESIDUAL = 0.5
NORM_EPS = 1e-6
NEG_INF = -1e30

LANES = 128
VMEM_LIMIT_BYTES = 58 * 1024 * 1024


def _cparams(semantics):
    return pltpu.CompilerParams(dimension_semantics=semantics, vmem_limit_bytes=VMEM_LIMIT_BYTES)


def _sigmoid(x):
    return 1.0 / (1.0 + jnp.exp(-x))


def _silu(x):
    return x * _sigmoid(x)


def _dot(a, b):
    return jnp.dot(a, b, preferred_element_type=F32)


def _dot_nt(a, b):
    return lax.dot_general(a, b, (((1,), (1,)), ((), ())), preferred_element_type=F32)


def _wspec(lead, block, index_map):
    return pl.BlockSpec((None,) * len(lead) + block, lambda i, j: lead + index_map(i, j))


def _mod_norm(x, gain, scale, shift):
    y = x * lax.rsqrt(jnp.mean(x * x, axis=-1, keepdims=True) + NORM_EPS)
    return (y * gain) * (1.0 + scale) + shift


def _ada_kernel(c_ref, w_ref, b_ref, o_ref):
    cond = _silu(c_ref[...]).astype(BF16)
    o_ref[...] = _dot(cond, w_ref[...].astype(BF16)) + b_ref[...]


def _ada_proj(c, w_ada, b_ada, tn=1024):
    depth, d, n = w_ada.shape
    c8 = jnp.broadcast_to(c, (8, d))
    out = pl.pallas_call(
        _ada_kernel,
        grid=(depth, n // tn),
        in_specs=[
            pl.BlockSpec((8, d), lambda l, j: (0, 0)),
            pl.BlockSpec((None, d, tn), lambda l, j: (l, 0, j)),
            pl.BlockSpec((None, 1, tn), lambda l, j: (l, 0, j)),
        ],
        out_specs=pl.BlockSpec((None, 8, tn), lambda l, j: (l, 0, j)),
        out_shape=jax.ShapeDtypeStruct((depth, 8, n), F32),
        compiler_params=_cparams(("arbitrary", "arbitrary")),
        name="ada_proj",
    )(c8, w_ada, b_ada.reshape(depth, 1, n))
    return out[:, 0, :]


ROW_CHUNK = 128
NORM_ROWS = 128
COL_CHUNK = 512


def _ffn_kernel(x_ref, gain_ref, scale_ref, shift_ref, gate_ref, wg_ref, wu_ref, wo_ref,
                o_ref, h_scr):
    j = pl.program_id(1)
    tm, d = x_ref.shape

    @pl.when(j == 0)
    def _():
        def body(r, carry):
            rows = pl.ds(pl.multiple_of(r * NORM_ROWS, NORM_ROWS), NORM_ROWS)
            h = _mod_norm(x_ref[rows, :], gain_ref[...], scale_ref[...], shift_ref[...])
            h_scr[rows, :] = h.astype(BF16)
            o_ref[rows, :] = jnp.zeros((NORM_ROWS, d), F32)
            return carry
        lax.fori_loop(0, tm // NORM_ROWS, body, 0)

    h = h_scr[...]
    g = _dot(h, wg_ref[...].astype(BF16))
    u = _dot(h, wu_ref[...].astype(BF16))
    a = (_silu(g) * u).astype(BF16)
    for c in range(0, d, COL_CHUNK):
        o_ref[:, c:c + COL_CHUNK] += _dot(a, wo_ref[:, c:c + COL_CHUNK].astype(BF16))

    @pl.when(j == pl.num_programs(1) - 1)
    def _():
        def body(r, carry):
            rows = pl.ds(pl.multiple_of(r * ROW_CHUNK, ROW_CHUNK), ROW_CHUNK)
            o_ref[rows, :] = x_ref[rows, :] + (FFN_RESIDUAL * gate_ref[...]) * o_ref[rows, :]
            return carry
        lax.fori_loop(0, tm // ROW_CHUNK, body, 0)


def _ffn(x, gain, scale, shift, gate, w_in, w_out, lead, tm=1024, tf=256):
    s, d = x.shape
    f = w_out.shape[-2]
    tm = min(tm, s)
    nf = f // tf
    vec = pl.BlockSpec((1, d), lambda i, j: (0, 0))
    return pl.pallas_call(
        _ffn_kernel,
        grid=(s // tm, nf),
        in_specs=[
            pl.BlockSpec((tm, d), lambda i, j: (i, 0)),
            vec, vec, vec, vec,
            _wspec(lead, (d, tf), lambda i, j: (0, j)),
            _wspec(lead, (d, tf), lambda i, j: (0, j + nf)),
            _wspec(lead, (tf, d), lambda i, j: (j, 0)),
        ],
        out_specs=pl.BlockSpec((tm, d), lambda i, j: (i, 0)),
        out_shape=jax.ShapeDtypeStruct((s, d), F32),
        scratch_shapes=[pltpu.VMEM((tm, d), BF16)],
        compiler_params=_cparams(("arbitrary", "arbitrary")),
        name="ffn",
    )(x, gain, scale, shift, gate, w_in, w_in, w_out)


def _head_rms_norm(y, gain):
    outs = []
    for c in range(0, y.shape[1], ATTN_HEAD_DIM):
        yc = y[:, c:c + ATTN_HEAD_DIM]
        outs.append(yc * lax.rsqrt(jnp.mean(yc * yc, axis=-1, keepdims=True) + NORM_EPS))
    return jnp.concatenate(outs, axis=1) * gain


def _normproj_kernel(x_ref, gain_ref, scale_ref, shift_ref, w_ref, hgain_ref, o_ref, h_ref):
    j = pl.program_id(1)
    tm = x_ref.shape[0]

    @pl.when(j == 0)
    def _():
        def body(r, carry):
            rows = pl.ds(pl.multiple_of(r * NORM_ROWS, NORM_ROWS), NORM_ROWS)
            h = _mod_norm(x_ref[rows, :], gain_ref[...], scale_ref[...], shift_ref[...])
            h_ref[rows, :] = h.astype(BF16)
            return carry
        lax.fori_loop(0, tm // NORM_ROWS, body, 0)

    y = _dot_nt(h_ref[...], w_ref[...].astype(BF16))
    o_ref[...] = _head_rms_norm(y, hgain_ref[...]).astype(o_ref.dtype)


def _norm_qk_proj(x, gain, scale, shift, wt, lead, hgain, tm=1024, tn=512):
    s, d = x.shape
    n = 2 * ATTN_WIDTH
    tm = min(tm, s)
    vec = pl.BlockSpec((1, d), lambda i, j: (0, 0))
    return pl.pallas_call(
        _normproj_kernel,
        grid=(s // tm, n // tn),
        in_specs=[
            pl.BlockSpec((tm, d), lambda i, j: (i, 0)),
            vec, vec, vec,
            _wspec(lead, (tn, d), lambda i, j: (j, 0)),
            pl.BlockSpec((1, tn), lambda i, j: (0, j)),
        ],
        out_specs=[
            pl.BlockSpec((tm, tn), lambda i, j: (i, j)),
            pl.BlockSpec((tm, d), lambda i, j: (i, 0)),
        ],
        out_shape=[jax.ShapeDtypeStruct((s, n), BF16), jax.ShapeDtypeStruct((s, d), BF16)],
        compiler_params=_cparams(("arbitrary", "arbitrary")),
        name="norm_qk_proj",
    )(x, gain, scale, shift, wt, hgain)


def _proj_kernel(h_ref, *refs, shift):
    wa_ref, o_ref, w_scr = refs[0], refs[-2], refs[-1]
    tn = wa_ref.shape[0]

    @pl.when(pl.program_id(1) == 0)
    def _():
        w_scr[:tn - shift, :] = wa_ref[shift:, :].astype(BF16)
        if shift:
            w_scr[tn - shift:, :] = refs[1][:shift, :].astype(BF16)

    o_ref[...] = _dot_nt(h_ref[...], w_scr[...]).astype(o_ref.dtype)


PROJ_SHIFT_ROWS = 128


def _proj(h, wt, lead, row0, n, out_dtype, tm=1024, tn=1024):
    s, d = h.shape
    tm = min(tm, s)
    tn = min(tn, n)
    base = row0 // tn * tn
    shift = row0 - base
    assert n % tn == 0 and shift % 16 == 0 and shift <= PROJ_SHIFT_ROWS and tn % PROJ_SHIFT_ROWS == 0
    w_specs = [_wspec(lead, (tn, d), lambda j, i: (base // tn + j, 0))]
    if shift:
        w_specs.append(_wspec(lead, (PROJ_SHIFT_ROWS, d),
                              lambda j, i: ((base + (j + 1) * tn) // PROJ_SHIFT_ROWS, 0)))
    return pl.pallas_call(
        functools.partial(_proj_kernel, shift=shift),
        grid=(n // tn, s // tm),
        in_specs=[pl.BlockSpec((tm, d), lambda j, i: (i, 0))] + w_specs,
        out_specs=pl.BlockSpec((tm, tn), lambda j, i: (i, j)),
        out_shape=jax.ShapeDtypeStruct((s, n), out_dtype),
        scratch_shapes=[pltpu.VMEM((tn, d), BF16)],
        compiler_params=_cparams(("arbitrary", "arbitrary")),
        name="proj",
    )(h, *([wt] * len(w_specs)))


def _t5_bucket(dist):
    n = jnp.maximum(dist, 0)
    max_exact = REL_BUCKETS // 2
    ratio = jnp.log(jnp.maximum(n, 1).astype(F32) / max_exact) / math.log(REL_MAX_DIST / max_exact)
    large = max_exact + (ratio * (REL_BUCKETS - max_exact)).astype(jnp.int32)
    large = jnp.minimum(large, REL_BUCKETS - 1)
    return jnp.where(n < max_exact, n, large)


ATTN_VIS_STEPS = (4, 8, 12, 16, 20, 24, 28, 32)
ATTN_QK_AHEAD = 5
ATTN_HEADS_PER_STEP = 2
ATTN_SUM_ROWS = 16
LOG2E = 1.4426950408889634
ATTN_Q_PRESCALE = ATTN_HEAD_DIM ** -0.5 * LOG2E


def _moba_kernel(rel_ref, bkt_own_ref, bkt_prev_ref, q_ref, k_ref, v_ref, o_ref,
                 kmean_scr, bias_own_scr, bias_prev_scr, vt_scr, mask_scr):
    hg = pl.program_id(0)
    i = pl.program_id(1)
    blk, hd, G = MOBA_BLOCK, ATTN_HEAD_DIM, ATTN_HEADS_PER_STEP
    nb = k_ref.shape[0] // blk
    heads = [hg * G + g for g in range(G)]
    cols = [slice(g * hd, (g + 1) * hd) for g in range(G)]

    @pl.when(i == 0)
    def _():
        kmean_scr[...] = jnp.zeros(kmean_scr.shape, F32)
        ones_row = jnp.where(lax.broadcasted_iota(jnp.int32, (ATTN_SUM_ROWS, blk), 0) == 0, 1.0, 0.0)
        key = lax.broadcasted_iota(jnp.int32, (blk, blk), 0)
        qry = lax.broadcasted_iota(jnp.int32, (blk, blk), 1)
        for g in range(G):
            def block_body(b, carry, g=g):
                rows = pl.ds(pl.multiple_of(b * blk, blk), blk)
                kmean_scr[g, pl.ds(b, 1), :] = jnp.mean(k_ref[rows, cols[g]].astype(F32), axis=0, keepdims=True)
                vt_scr[g, b, :hd, :] = v_ref[rows, cols[g]].astype(F32).T.astype(BF16)
                vt_scr[g, b, hd:, :] = ones_row.astype(BF16)
                return carry
            lax.fori_loop(0, nb, block_body, 0)

            bias_own_scr[g] = jnp.where(key <= qry, 0.0, NEG_INF)
            bias_prev_scr[g] = jnp.zeros((blk, blk), F32)

            def bias_body(b, carry, g=g):
                val = rel_ref[b, heads[g]] * LOG2E
                bias_own_scr[g] += jnp.where(bkt_own_ref[...] == b, val, 0.0)
                bias_prev_scr[g] += jnp.where(bkt_prev_ref[...] == b, val, 0.0)
                return carry
            lax.fori_loop(0, REL_BUCKETS, bias_body, 0)

    qs = [q_ref[:, cols[g]] for g in range(G)]

    for g in range(G):
        km = kmean_scr[g]
        km_hi = km.astype(BF16)
        km_lo = (km - km_hi.astype(F32)).astype(BF16)
        gate = _dot_nt(km_hi, qs[g]) + _dot_nt(km_lo, qs[g])
        rowb = lax.broadcasted_iota(jnp.int32, gate.shape, 0)
        rowf = rowb.astype(F32)
        gate = jnp.where(rowb < i, gate, -jnp.inf)
        sel = jnp.zeros(gate.shape, F32)
        for _ in range(min(MOBA_TOPK, nb)):
            top = jnp.max(gate, axis=0, keepdims=True)
            idx = jnp.min(jnp.where(gate == top, rowf, float(LANES)), axis=0, keepdims=True)
            hit = rowf == idx
            sel = jnp.where(hit & (top > -jnp.inf), 1.0, sel)
            gate = jnp.where(hit, -jnp.inf, gate)
        mask_scr[g] = jnp.where(sel > 0.0, 0.0, NEG_INF)

    far_bias = [rel_ref[REL_BUCKETS - 1, heads[g]] * LOG2E for g in range(G)]
    visible = i + 1
    lo = 0
    for n_vis in sorted({min(v, nb) for v in ATTN_VIS_STEPS} | {nb}):
        @pl.when((visible > lo) & (visible <= n_vis))
        def _(n_vis=n_vis):
            _moba_tile(n_vis, i, far_bias, qs, cols, k_ref, o_ref, bias_own_scr, bias_prev_scr, vt_scr, mask_scr)
        lo = n_vis


def _moba_tile(n_vis, i, far_bias, qs, cols, k_ref, o_ref, bias_own_scr, bias_prev_scr, vt_scr, mask_scr):
    blk, hd = MOBA_BLOCK, ATTN_HEAD_DIM
    heads = range(len(qs))
    qt = [q.astype(F32).T.astype(BF16) for q in qs]

    def rows(j):
        return pl.ds(pl.multiple_of(j * blk, blk), blk)

    n_far = n_vis - 2
    jp = jnp.maximum(i - 1, 0)
    s_own = [_dot(k_ref[rows(i), cols[g]], qt[g]) for g in heads]
    s_prev = [_dot(k_ref[rows(jp), cols[g]], qt[g]) for g in heads]

    def far_qk(j):
        return [_dot(k_ref[j * blk:(j + 1) * blk, cols[g]], qt[g]) for g in heads]
    far_s = {j: far_qk(j) for j in range(min(ATTN_QK_AHEAD, n_far))}

    m, acc = [], []
    for g in heads:
        chosen = jnp.where(i >= 1, mask_scr[g, pl.ds(jp, 1), :], NEG_INF)
        t_own = s_own[g] + bias_own_scr[g]
        t_prev = s_prev[g] + bias_prev_scr[g] + chosen
        m.append(jnp.maximum(jnp.max(t_own, axis=0, keepdims=True), jnp.max(t_prev, axis=0, keepdims=True)))
        p = jnp.concatenate([jnp.exp2(t_prev - m[g]), jnp.exp2(t_own - m[g])], axis=0).astype(BF16)
        acc.append(_dot(jnp.concatenate([vt_scr[g, jp], vt_scr[g, i]], axis=1), p))

    for j in range(n_far):
        s = far_s.pop(j)
        if j + ATTN_QK_AHEAD < n_far:
            far_s[j + ATTN_QK_AHEAD] = far_qk(j + ATTN_QK_AHEAD)
        for g in heads:
            shift = far_bias[g] + jnp.where(j < i - 1, mask_scr[g, j:j + 1, :], NEG_INF)
            m_new = jnp.maximum(m[g], jnp.max(s[g], axis=0, keepdims=True) + shift)
            p = jnp.exp2(s[g] + (shift - m_new))
            acc[g] = jnp.exp2(m[g] - m_new) * acc[g] + _dot(vt_scr[g, j], p.astype(BF16))
            m[g] = m_new

    for g in heads:
        o_ref[:, cols[g]] = (acc[g][:hd, :] / acc[g][hd:hd + 1, :]).T.astype(o_ref.dtype)


def _moba_attention(qk, v, rel_bias):
    s = qk.shape[0]
    blk, hd, nh = MOBA_BLOCK, ATTN_HEAD_DIM, ATTN_HEADS
    assert s % blk == 0 and s // blk < LANES
    assert blk >= REL_MAX_DIST
    r = jnp.arange(blk)
    dist_own = r[None, :] - r[:, None]
    bkt_own = _t5_bucket(dist_own)
    bkt_prev = _t5_bucket(dist_own + blk)
    const = lambda h, i: (0, 0)
    G = ATTN_HEADS_PER_STEP
    assert nh % G == 0
    gw = G * hd
    return pl.pallas_call(
        _moba_kernel,
        grid=(nh // G, s // blk),
        in_specs=[
            pl.BlockSpec(memory_space=pltpu.SMEM),
            pl.BlockSpec((blk, blk), const),
            pl.BlockSpec((blk, blk), const),
            pl.BlockSpec((blk, gw), lambda h, i: (i, h)),
            pl.BlockSpec((s, gw), lambda h, i: (0, nh // G + h)),
            pl.BlockSpec((s, gw), lambda h, i: (0, h)),
        ],
        out_specs=pl.BlockSpec((blk, gw), lambda h, i: (i, h)),
        out_shape=jax.ShapeDtypeStruct((s, nh * hd), BF16),
        scratch_shapes=[
            pltpu.VMEM((G, LANES, hd), F32),
            pltpu.VMEM((G, blk, blk), F32),
            pltpu.VMEM((G, blk, blk), F32),
            pltpu.VMEM((G, s // blk, hd + ATTN_SUM_ROWS, blk), BF16),
            pltpu.VMEM((G, LANES, blk), F32),
        ],
        compiler_params=_cparams(("arbitrary", "arbitrary")),
        name="moba_attention",
    )(rel_bias, bkt_own, bkt_prev, qk, qk, v)


SSD_TAIL = 8
SSD_CONV_COLS = 512


def _softplus(x):
    return jnp.maximum(x, 0.0) + jnp.log(1.0 + jnp.exp(-jnp.abs(x)))


def _split3(x):
    hi = x.astype(BF16)
    r1 = x - hi.astype(F32)
    mid = r1.astype(BF16)
    lo = (r1 - mid.astype(F32)).astype(BF16)
    return hi, mid, lo


def _ssd_kernel(xbc_ref, z_ref, dt_ref, dtt_ref, wconv_ref, bconv_ref, dtb_ref, alog_ref,
                dtbt_ref, alogt_ref, dskip_ref, norm_ref, y_ref,
                tail_scr, xact_scr, state_scr):
    c = pl.program_id(0)
    L, P, N, W = SSD_CHUNK, SSD_HEAD_DIM, SSD_STATE, SSD_WIDTH
    assert P & (P - 1) == 0

    @pl.when(c == 0)
    def _():
        tail_scr[...] = jnp.zeros(tail_scr.shape, F32)
        state_scr[...] = jnp.zeros(state_scr.shape, F32)

    rr = lax.broadcasted_iota(jnp.int32, (L, L), 0)
    cc = lax.broadcasted_iota(jnp.int32, (L, L), 1)
    causal = rr >= cc

    shifts = [jnp.where(rr - cc == k, 1.0, 0.0).astype(BF16) for k in range(1, SSD_CONV)]
    row8 = lax.broadcasted_iota(jnp.int32, (SSD_TAIL, SSD_CONV_COLS), 0)
    for c0 in range(0, SSD_XBC, SSD_CONV_COLS):
        cols = slice(c0, c0 + SSD_CONV_COLS)
        x = xbc_ref[:, cols]
        tail = tail_scr[:, cols]
        acc = bconv_ref[:, cols] + wconv_ref[SSD_CONV - 1:SSD_CONV, cols] * x.astype(F32)
        fix = jnp.zeros((SSD_TAIL, SSD_CONV_COLS), F32)
        for k in range(1, SSD_CONV):
            wk = wconv_ref[SSD_CONV - 1 - k:SSD_CONV - k, cols]
            acc = acc + wk * _dot(shifts[k - 1], x)
            fix = fix + wk * jnp.where(row8 < k, pltpu.roll(tail, k, 0), 0.0)
        acc = jnp.concatenate([acc[:SSD_TAIL] + fix, acc[SSD_TAIL:]], axis=0)
        xact_scr[:, cols] = _silu(acc)
        tail_scr[:, cols] = x[L - SSD_TAIL:, :].astype(F32)

    dtv = _softplus(dt_ref[...] + dtb_ref[...])
    ad = dtv * (-jnp.exp(alog_ref[...]))
    dtt = _softplus(dtt_ref[...] + dtbt_ref[...])
    adt = dtt * (-jnp.exp(alogt_ref[...]))
    lower = jnp.where(causal, 1.0, 0.0).astype(BF16)
    upper = jnp.where(rr <= cc, 1.0, 0.0).astype(BF16)
    a_cs = sum(_dot(lower, t) for t in _split3(ad))
    a_cst = sum(_dot(t, upper) for t in _split3(adt))
    last = a_cs[L - 1:L, :]

    factors = jnp.concatenate([dtv, jnp.exp(a_cs), jnp.exp(last - a_cs)], axis=0).astype(BF16)
    group_w = W // SSD_GROUPS
    heads_per_group = SSD_HEADS // SSD_GROUPS
    e_row = lax.broadcasted_iota(jnp.int32, (LANES, group_w), 0)
    e_head = lax.shift_right_logical(lax.broadcasted_iota(jnp.int32, (LANES, group_w), 1),
                                     P.bit_length() - 1)
    pair_lane = lax.broadcasted_iota(jnp.int32, (L, 2 * P), 1)

    for g in range(SSD_GROUPS):
        gc = slice(g * group_w, (g + 1) * group_w)
        expand = jnp.where(e_row == g * heads_per_group + e_head, 1.0, 0.0).astype(BF16)
        spread = _dot(factors, expand)
        xs = xact_scr[:, gc]
        xd = xs * spread[:L]
        xd_b = xd.astype(BF16)
        xe_b = (xd * spread[2 * L:]).astype(BF16)
        bg = xact_scr[:, W + g * N:W + (g + 1) * N].astype(BF16)
        cg = xact_scr[:, W + SSD_BC + g * N:W + SSD_BC + (g + 1) * N].astype(BF16)
        st = state_scr[gc, :]
        y_off = _dot_nt(cg, st.astype(BF16)) * spread[L:2 * L]
        new = lax.dot_general(xe_b, bg, (((0,), (0,)), ((), ())), preferred_element_type=F32)
        cb = _dot_nt(cg, bg)
        pairs = []
        for r in range(0, heads_per_group, 2):
            h = g * heads_per_group + r
            xd_pair = xd_b[:, r * P:(r + 2) * P]
            halves = []
            for hh in (h, h + 1):
                dec = jnp.exp(jnp.where(causal, a_cs[:, hh:hh + 1] - a_cst[hh:hh + 1, :], -jnp.inf))
                halves.append(_dot((cb * dec).astype(BF16), xd_pair))
                hr = slice((hh - g * heads_per_group) * P, (hh - g * heads_per_group + 1) * P)
                state_scr[g * group_w + hr.start:g * group_w + hr.stop, :] = (
                    jnp.exp(a_cst[hh:hh + 1, L - 1:L]) * st[hr, :] + new[hr, :])
            pairs.append(jnp.where(pair_lane < P, halves[0], halves[1]))
        y = jnp.concatenate(pairs, axis=1) + y_off + xs * dskip_ref[:, gc]
        y = y * _silu(z_ref[:, gc].astype(F32))
        y = y * lax.rsqrt(jnp.mean(y * y, axis=-1, keepdims=True) + NORM_EPS)
        y_ref[:, gc] = (y * norm_ref[:, gc]).astype(y_ref.dtype)


def _ssd(xbc, z, dt, w_conv, b_conv, dt_bias, a_log, d_skip, norm):
    s = xbc.shape[0]
    L, nh = SSD_CHUNK, SSD_HEADS
    assert s % L == 0
    dtt = dt[:, :nh].T
    pad = lambda v: jnp.pad(v, (0, LANES - nh)).reshape(1, LANES)
    full = lambda r, cdim: pl.BlockSpec((r, cdim), lambda c: (0, 0))
    return pl.pallas_call(
        _ssd_kernel,
        grid=(s // L,),
        in_specs=[
            pl.BlockSpec((L, SSD_XBC), lambda c: (c, 0)),
            pl.BlockSpec((L, SSD_WIDTH), lambda c: (c, 0)),
            pl.BlockSpec((L, LANES), lambda c: (c, 0)),
            pl.BlockSpec((nh, L), lambda c: (0, c)),
            full(SSD_CONV, SSD_XBC), full(1, SSD_XBC),
            full(1, LANES), full(1, LANES), full(nh, 1), full(nh, 1),
            full(1, SSD_WIDTH), full(1, SSD_WIDTH),
        ],
        out_specs=pl.BlockSpec((L, SSD_WIDTH), lambda c: (c, 0)),
        out_shape=jax.ShapeDtypeStruct((s, SSD_WIDTH), BF16),
        scratch_shapes=[
            pltpu.VMEM((SSD_TAIL, SSD_XBC), F32),
            pltpu.VMEM((L, SSD_XBC), F32),
            pltpu.VMEM((SSD_WIDTH, SSD_STATE), F32),
        ],
        compiler_params=_cparams(("arbitrary",)),
        name="ssd_scan",
    )(xbc, z, dt, dtt, w_conv, b_conv.reshape(1, SSD_XBC), pad(dt_bias), pad(a_log),
      dt_bias.reshape(nh, 1), a_log.reshape(nh, 1),
      jnp.repeat(d_skip, SSD_HEAD_DIM).reshape(1, SSD_WIDTH), norm.reshape(1, SSD_WIDTH))


HALO = 16


def _merge_kernel(ya_ref, ys_ref, cb_ref, cc_ref, cx_ref, hc_ref, hx_ref, ga_ref, gs_ref, gc_ref,
                  wsc_ref, wa_ref, ws_ref, wc_ref, o_ref, ext_scr, yc_scr):
    i = pl.program_id(0)
    j = pl.program_id(1)
    tm = ya_ref.shape[0]

    @pl.when(j == 0)
    def _():
        halo = hc_ref[...].astype(F32) * hx_ref[...].astype(F32)
        ext_scr[0:HALO, :] = jnp.where(i > 0, halo, 0.0)
        for r0 in range(0, tm, ROW_CHUNK):
            rows = slice(r0, r0 + ROW_CHUNK)
            ext_scr[HALO + r0:HALO + r0 + ROW_CHUNK, :] = cc_ref[rows, :].astype(F32) * cx_ref[rows, :].astype(F32)
        for r0 in range(0, tm, ROW_CHUNK):
            rows = slice(r0, r0 + ROW_CHUNK)
            acc = jnp.zeros((ROW_CHUNK, CONV_WIDTH), F32)
            for k in range(CONV_K):
                start = HALO + r0 - (CONV_K - 1) + k
                acc = acc + wsc_ref[k:k + 1, :] * ext_scr[start:start + ROW_CHUNK, :]
            yc_scr[rows, :] = (cb_ref[rows, :].astype(F32) * acc).astype(BF16)

    merged = (_sigmoid(ga_ref[...].astype(F32)) * _dot(ya_ref[...], wa_ref[...].astype(BF16))
              + _sigmoid(gs_ref[...].astype(F32)) * _dot(ys_ref[...], ws_ref[...].astype(BF16))
              + _sigmoid(gc_ref[...].astype(F32)) * _dot(yc_scr[...], wc_ref[...].astype(BF16)))
    o_ref[...] = merged.astype(o_ref.dtype)


def _branch_merge(y_attn, y_ssd, tail, w_sc, w_a, w_s, w_c, lead, tm=1024, tn=512):
    s = y_attn.shape[0]
    d = w_a.shape[-1]
    tm = min(tm, s)
    cw = CONV_WIDTH
    g0 = 3 * cw // tn
    gd = d // tn
    row = lambda width, cb: pl.BlockSpec((tm, width), lambda i, j: (i, cb))
    halo = lambda cb: pl.BlockSpec((HALO, cw), lambda i, j: (jnp.maximum(i * (tm // HALO) - 1, 0), cb))
    gate = lambda k: pl.BlockSpec((tm, tn), lambda i, j: (i, g0 + k * gd + j))
    wcol = lambda kdim: _wspec(lead, (kdim, tn), lambda i, j: (0, j))
    return pl.pallas_call(
        _merge_kernel,
        grid=(s // tm, d // tn),
        in_specs=[
            row(ATTN_WIDTH, 0), row(SSD_WIDTH, 0),
            row(cw, 0), row(cw, 1), row(cw, 2), halo(1), halo(2),
            gate(0), gate(1), gate(2),
            pl.BlockSpec((CONV_K, cw), lambda i, j: (0, 0)),
            wcol(ATTN_WIDTH), wcol(SSD_WIDTH), wcol(cw),
        ],
        out_specs=pl.BlockSpec((tm, tn), lambda i, j: (i, j)),
        out_shape=jax.ShapeDtypeStruct((s, d), BF16),
        scratch_shapes=[pltpu.VMEM((HALO + tm, cw), F32), pltpu.VMEM((tm, cw), BF16)],
        compiler_params=_cparams(("arbitrary", "arbitrary")),
        name="branch_merge",
    )(y_attn, y_ssd, tail, tail, tail, tail, tail, tail, tail, tail, w_sc, w_a, w_s, w_c)


def _outproj_kernel(m_ref, w_ref, x_ref, gate_ref, o_ref):
    o_ref[...] = x_ref[...] + gate_ref[...] * _dot(m_ref[...], w_ref[...].astype(BF16))


def _out_proj(merged, w, lead, x, gate, tm=1024, tn=1024):
    s, d = x.shape
    tm = min(tm, s)
    return pl.pallas_call(
        _outproj_kernel,
        grid=(d // tn, s // tm),
        in_specs=[
            pl.BlockSpec((tm, merged.shape[1]), lambda j, i: (i, 0)),
            _wspec(lead, (merged.shape[1], tn), lambda j, i: (0, j)),
            pl.BlockSpec((tm, tn), lambda j, i: (i, j)),
            pl.BlockSpec((1, tn), lambda j, i: (0, j)),
        ],
        out_specs=pl.BlockSpec((tm, tn), lambda j, i: (i, j)),
        out_shape=jax.ShapeDtypeStruct((s, d), F32),
        compiler_params=_cparams(("arbitrary", "arbitrary")),
        name="out_proj",
    )(merged, w, x, gate)


def _token_mix(x, gain, scale, shift, gate, l, w_mix_in, qk_norm, rel_bias, w_ssd_conv, b_ssd_conv,
               ssd_dt_bias, ssd_a_log, ssd_d, ssd_norm, w_sc_conv, w_br_attn, w_br_ssd, w_br_conv,
               w_mix_out):
    d = x.shape[1]
    lead = (l,)
    c_v = 2 * ATTN_WIDTH
    c_z = c_v + ATTN_WIDTH
    c_xbc = c_z + SSD_WIDTH
    c_dt = c_xbc + SSD_XBC
    c_tail = c_dt + SSD_HEADS
    n_tail = 3 * CONV_WIDTH + 3 * d
    assert w_mix_in.shape[-1] == c_tail + n_tail

    hgain = jnp.concatenate([jnp.tile(qk_norm[l, 0] * ATTN_Q_PRESCALE, ATTN_HEADS),
                             jnp.tile(qk_norm[l, 1], ATTN_HEADS)])[None]
    wt = jnp.swapaxes(w_mix_in, 1, 2)
    qk, h = _norm_qk_proj(x, gain, scale, shift, wt, lead, hgain)
    v = _proj(h, wt, lead, c_v, ATTN_WIDTH, BF16)
    z = _proj(h, wt, lead, c_z, SSD_WIDTH, BF16)
    xbc = _proj(h, wt, lead, c_xbc, SSD_XBC, BF16)
    wt_dt = jnp.pad(wt[l, c_dt:c_tail, :], ((0, LANES - SSD_HEADS), (0, 0)))
    dt = _proj(h, wt_dt, (), 0, LANES, F32)
    tail = _proj(h, wt, lead, c_tail, n_tail, BF16)

    y_attn = _moba_attention(qk, v, rel_bias)
    y_ssd = _ssd(xbc, z, dt, w_ssd_conv[l], b_ssd_conv[l], ssd_dt_bias[l], ssd_a_log[l], ssd_d[l], ssd_norm[l])
    merged = _branch_merge(y_attn, y_ssd, tail, w_sc_conv[l], w_br_attn[l].astype(BF16),
                           w_br_ssd[l].astype(BF16), w_br_conv[l].astype(BF16), ())
    return _out_proj(merged, w_mix_out, lead, x, gate)


def kernel(x, c, w_ada, b_ada, norm_gain, w_ffn_in, w_ffn_out, w_mix_in, qk_norm, rel_bias, w_ssd_conv, b_ssd_conv, ssd_dt_bias, ssd_a_log, ssd_d, ssd_norm, w_sc_conv, w_br_attn, w_br_ssd, w_br_conv, w_mix_out):
    b, s, d = x.shape
    depth = w_ada.shape[0]
    assert b == 1 and s % math.lcm(MOBA_BLOCK, SSD_CHUNK) == 0
    xs = x.reshape(s, d)
    ada = _ada_proj(c, w_ada, b_ada).reshape(depth, N_SUBLAYERS, 3, 1, d)
    for l in range(depth):
        mod = lambda i: (norm_gain[l, i][None], ada[l, i, 1], ada[l, i, 0], ada[l, i, 2])
        xs = _ffn(xs, *mod(0), w_ffn_in, w_ffn_out, (l, 0))
        xs = _token_mix(xs, *mod(1), l, w_mix_in, qk_norm, rel_bias, w_ssd_conv, b_ssd_conv,
                        ssd_dt_bias, ssd_a_log, ssd_d, ssd_norm, w_sc_conv,
                        w_br_attn, w_br_ssd, w_br_conv, w_mix_out)
        xs = _ffn(xs, *mod(2), w_ffn_in, w_ffn_out, (l, 1))
    return xs.reshape(b, s, d)
```

```python
import functools
import math

import jax
import jax.numpy as jnp
from jax import lax
from jax.experimental import pallas as pl
from jax.experimental.pallas import tpu as pltpu

F32 = jnp.float32
BF16 = jnp.bfloat16

ATTN_HEADS = 8
ATTN_HEAD_DIM = 128
ATTN_WIDTH = ATTN_HEADS * ATTN_HEAD_DIM
MOBA_BLOCK = 256
MOBA_TOPK = 3
REL_BUCKETS = 32
REL_MAX_DIST = 128
SSD_HEADS = 32
SSD_HEAD_DIM = 64
SSD_WIDTH = SSD_HEADS * SSD_HEAD_DIM
SSD_GROUPS = 4
SSD_STATE = 128
SSD_CONV = 4
SSD_CHUNK = 256
SSD_BC = SSD_GROUPS * SSD_STATE
SSD_XBC = SSD_WIDTH + 2 * SSD_BC
CONV_WIDTH = 1024
CONV_K = 3
N_SUBLAYERS = 3
FFN_RESIDUAL = 0.5
NORM_EPS = 1e-6
NEG_INF = -1e30

LANES = 128
VMEM_LIMIT_BYTES = 58 * 1024 * 1024


def _cparams(semantics):
    return pltpu.CompilerParams(dimension_semantics=semantics, vmem_limit_bytes=VMEM_LIMIT_BYTES)


def _sigmoid(x):
    return 1.0 / (1.0 + jnp.exp(-x))


def _silu(x):
    return x * _sigmoid(x)


def _dot(a, b):
    return jnp.dot(a, b, preferred_element_type=F32)


def _dot_nt(a, b):
    return lax.dot_general(a, b, (((1,), (1,)), ((), ())), preferred_element_type=F32)


def _wspec(lead, block, index_map):
    return pl.BlockSpec((None,) * len(lead) + block, lambda i, j: lead + index_map(i, j))


def _mod_norm(x, gain, scale, shift):
    y = x * lax.rsqrt(jnp.mean(x * x, axis=-1, keepdims=True) + NORM_EPS)
    return (y * gain) * (1.0 + scale) + shift


def _ada_kernel(c_ref, w_ref, b_ref, o_ref):
    cond = _silu(c_ref[...]).astype(BF16)
    o_ref[...] = _dot(cond, w_ref[...].astype(BF16)) + b_ref[...]


def _ada_proj(c, w_ada, b_ada, tn=1024):
    depth, d, n = w_ada.shape
    c8 = jnp.broadcast_to(c, (8, d))
    out = pl.pallas_call(
        _ada_kernel,
        grid=(depth, n // tn),
        in_specs=[
            pl.BlockSpec((8, d), lambda l, j: (0, 0)),
            pl.BlockSpec((None, d, tn), lambda l, j: (l, 0, j)),
            pl.BlockSpec((None, 1, tn), lambda l, j: (l, 0, j)),
        ],
        out_specs=pl.BlockSpec((None, 8, tn), lambda l, j: (l, 0, j)),
        out_shape=jax.ShapeDtypeStruct((depth, 8, n), F32),
        compiler_params=_cparams(("arbitrary", "arbitrary")),
        name="ada_proj",
    )(c8, w_ada, b_ada.reshape(depth, 1, n))
    return out[:, 0, :]


ROW_CHUNK = 128
NORM_ROWS = 128
COL_CHUNK = 512


def _ffn_kernel(x_ref, gain_ref, scale_ref, shift_ref, gate_ref, wg_ref, wu_ref, wo_ref,
                o_ref, h_scr):
    j = pl.program_id(1)
    tm, d = x_ref.shape

    @pl.when(j == 0)
    def _():
        def body(r, carry):
            rows = pl.ds(pl.multiple_of(r * NORM_ROWS, NORM_ROWS), NORM_ROWS)
            h = _mod_norm(x_ref[rows, :], gain_ref[...], scale_ref[...], shift_ref[...])
            h_scr[rows, :] = h.astype(BF16)
            o_ref[rows, :] = jnp.zeros((NORM_ROWS, d), F32)
            return carry
        lax.fori_loop(0, tm // NORM_ROWS, body, 0)

    h = h_scr[...]
    g = _dot(h, wg_ref[...].astype(BF16))
    u = _dot(h, wu_ref[...].astype(BF16))
    a = (_silu(g) * u).astype(BF16)
    for c in range(0, d, COL_CHUNK):
        o_ref[:, c:c + COL_CHUNK] += _dot(a, wo_ref[:, c:c + COL_CHUNK].astype(BF16))

    @pl.when(j == pl.num_programs(1) - 1)
    def _():
        def body(r, carry):
            rows = pl.ds(pl.multiple_of(r * ROW_CHUNK, ROW_CHUNK), ROW_CHUNK)
            o_ref[rows, :] = x_ref[rows, :] + (FFN_RESIDUAL * gate_ref[...]) * o_ref[rows, :]
            return carry
        lax.fori_loop(0, tm // ROW_CHUNK, body, 0)


def _ffn(x, gain, scale, shift, gate, w_in, w_out, lead, tm=1024, tf=256):
    s, d = x.shape
    f = w_out.shape[-2]
    tm = min(tm, s)
    nf = f // tf
    vec = pl.BlockSpec((1, d), lambda i, j: (0, 0))
    return pl.pallas_call(
        _ffn_kernel,
        grid=(s // tm, nf),
        in_specs=[
            pl.BlockSpec((tm, d), lambda i, j: (i, 0)),
            vec, vec, vec, vec,
            _wspec(lead, (d, tf), lambda i, j: (0, j)),
            _wspec(lead, (d, tf), lambda i, j: (0, j + nf)),
            _wspec(lead, (tf, d), lambda i, j: (j, 0)),
        ],
        out_specs=pl.BlockSpec((tm, d), lambda i, j: (i, 0)),
        out_shape=jax.ShapeDtypeStruct((s, d), F32),
        scratch_shapes=[pltpu.VMEM((tm, d), BF16)],
        compiler_params=_cparams(("arbitrary", "arbitrary")),
        name="ffn",
    )(x, gain, scale, shift, gate, w_in, w_in, w_out)


def _head_rms_norm(y, gain):
    outs = []
    for c in range(0, y.shape[1], ATTN_HEAD_DIM):
        yc = y[:, c:c + ATTN_HEAD_DIM]
        outs.append(yc * lax.rsqrt(jnp.mean(yc * yc, axis=-1, keepdims=True) + NORM_EPS))
    return jnp.concatenate(outs, axis=1) * gain


def _norm_kernel(x_ref, gain_ref, scale_ref, shift_ref, h_ref):
    def body(r, carry):
        rows = pl.ds(pl.multiple_of(r * NORM_ROWS, NORM_ROWS), NORM_ROWS)
        h = _mod_norm(x_ref[rows, :], gain_ref[...], scale_ref[...], shift_ref[...])
        h_ref[rows, :] = h.astype(BF16)
        return carry
    lax.fori_loop(0, x_ref.shape[0] // NORM_ROWS, body, 0)


def _norm(x, gain, scale, shift, tm=512):
    s, d = x.shape
    tm = min(tm, s)
    vec = pl.BlockSpec((1, d), lambda i: (0, 0))
    return pl.pallas_call(
        _norm_kernel,
        grid=(s // tm,),
        in_specs=[pl.BlockSpec((tm, d), lambda i: (i, 0)), vec, vec, vec],
        out_specs=pl.BlockSpec((tm, d), lambda i: (i, 0)),
        out_shape=jax.ShapeDtypeStruct((s, d), BF16),
        compiler_params=_cparams(("arbitrary",)),
        name="mod_norm",
    )(x, gain, scale, shift)


def _proj_kernel(h_ref, *refs, shift, head_norm):
    wa_ref, o_ref, w_scr = refs[0], refs[-2], refs[-1]
    tn = wa_ref.shape[0]

    @pl.when(pl.program_id(1) == 0)
    def _():
        w_scr[:tn - shift, :] = wa_ref[shift:, :].astype(BF16)
        if shift:
            w_scr[tn - shift:, :] = refs[1][:shift, :].astype(BF16)

    y = _dot_nt(h_ref[...], w_scr[...])
    if head_norm:
        y = _head_rms_norm(y, refs[-3][...])
    o_ref[...] = y.astype(o_ref.dtype)


PROJ_SHIFT_ROWS = 128


def _proj(h, wt, lead, row0, n, out_dtype, head_gain=None, tm=1024, tn=1024):
    s, d = h.shape
    tm = min(tm, s)
    tn = min(tn, n)
    base = row0 // tn * tn
    shift = row0 - base
    assert n % tn == 0 and shift % 16 == 0 and shift <= PROJ_SHIFT_ROWS and tn % PROJ_SHIFT_ROWS == 0
    specs = [pl.BlockSpec((tm, d), lambda j, i: (i, 0)),
             _wspec(lead, (tn, d), lambda j, i: (base // tn + j, 0))]
    args = [h, wt]
    if shift:
        specs.append(_wspec(lead, (PROJ_SHIFT_ROWS, d),
                            lambda j, i: ((base + (j + 1) * tn) // PROJ_SHIFT_ROWS, 0)))
        args.append(wt)
    if head_gain is not None:
        specs.append(pl.BlockSpec((1, tn), lambda j, i: (0, j)))
        args.append(head_gain)
    return pl.pallas_call(
        functools.partial(_proj_kernel, shift=shift, head_norm=head_gain is not None),
        grid=(n // tn, s // tm),
        in_specs=specs,
        out_specs=pl.BlockSpec((tm, tn), lambda j, i: (i, j)),
        out_shape=jax.ShapeDtypeStruct((s, n), out_dtype),
        scratch_shapes=[pltpu.VMEM((tn, d), BF16)],
        compiler_params=_cparams(("arbitrary", "arbitrary")),
        name="proj",
    )(*args)


def _t5_bucket(dist):
    n = jnp.maximum(dist, 0)
    max_exact = REL_BUCKETS // 2
    ratio = jnp.log(jnp.maximum(n, 1).astype(F32) / max_exact) / math.log(REL_MAX_DIST / max_exact)
    large = max_exact + (ratio * (REL_BUCKETS - max_exact)).astype(jnp.int32)
    large = jnp.minimum(large, REL_BUCKETS - 1)
    return jnp.where(n < max_exact, n, large)


ATTN_VIS_STEPS = (4, 8, 12, 16, 20, 24, 28, 32)
ATTN_QK_AHEAD = 2
ATTN_HEADS_PER_STEP = 2
ATTN_SUM_ROWS = 16
LOG2E = 1.4426950408889634
ATTN_Q_PRESCALE = ATTN_HEAD_DIM ** -0.5 * LOG2E


def _moba_kernel(rel_ref, bkt_own_ref, bkt_prev_ref, q_ref, k_ref, v_ref, o_ref,
                 kmean_scr, bias_own_scr, bias_prev_scr, vt_scr, mask_scr):
    hg = pl.program_id(0)
    i = pl.program_id(1)
    blk, hd, G = MOBA_BLOCK, ATTN_HEAD_DIM, ATTN_HEADS_PER_STEP
    nb = k_ref.shape[0] // blk
    heads = [hg * G + g for g in range(G)]
    cols = [slice(g * hd, (g + 1) * hd) for g in range(G)]

    @pl.when(i == 0)
    def _():
        kmean_scr[...] = jnp.zeros(kmean_scr.shape, F32)
        ones_row = jnp.where(lax.broadcasted_iota(jnp.int32, (ATTN_SUM_ROWS, blk), 0) == 0, 1.0, 0.0)
        key = lax.broadcasted_iota(jnp.int32, (blk, blk), 0)
        qry = lax.broadcasted_iota(jnp.int32, (blk, blk), 1)
        for g in range(G):
            def block_body(b, carry, g=g):
                rows = pl.ds(pl.multiple_of(b * blk, blk), blk)
                kmean_scr[g, pl.ds(b, 1), :] = jnp.mean(k_ref[rows, cols[g]].astype(F32), axis=0, keepdims=True)
                vt_scr[g, b, :hd, :] = v_ref[rows, cols[g]].astype(F32).T.astype(BF16)
                vt_scr[g, b, hd:, :] = ones_row.astype(BF16)
                return carry
            lax.fori_loop(0, nb, block_body, 0)

            bias_own_scr[g] = jnp.where(key <= qry, 0.0, NEG_INF)
            bias_prev_scr[g] = jnp.zeros((blk, blk), F32)

            def bias_body(b, carry, g=g):
                val = rel_ref[b, heads[g]] * LOG2E
                bias_own_scr[g] += jnp.where(bkt_own_ref[...] == b, val, 0.0)
                bias_prev_scr[g] += jnp.where(bkt_prev_ref[...] == b, val, 0.0)
                return carry
            lax.fori_loop(0, REL_BUCKETS, bias_body, 0)

    qs = [q_ref[:, cols[g]] for g in range(G)]
    for g in range(G):
        mask_scr[g] = _choose_blocks(i, nb, qs[g], kmean_scr[g])

    far_bias = [rel_ref[REL_BUCKETS - 1, heads[g]] * LOG2E for g in range(G)]
    visible = i + 1
    lo = 0
    for n_vis in sorted({min(v, nb) for v in ATTN_VIS_STEPS} | {nb}):
        @pl.when((visible > lo) & (visible <= n_vis))
        def _(n_vis=n_vis):
            _moba_tile(n_vis, nb, i, far_bias, qs, cols, k_ref, o_ref,
                       kmean_scr, bias_own_scr, bias_prev_scr, vt_scr, mask_scr)
        lo = n_vis


def _choose_blocks(i, nb, q, kmean):
    km_hi = kmean.astype(BF16)
    km_lo = (kmean - km_hi.astype(F32)).astype(BF16)
    gate = _dot_nt(km_hi, q) + _dot_nt(km_lo, q)
    rowb = lax.broadcasted_iota(jnp.int32, gate.shape, 0)
    rowf = rowb.astype(F32)
    gate = jnp.where(rowb < i, gate, -jnp.inf)
    sel = jnp.zeros(gate.shape, F32)
    for _ in range(min(MOBA_TOPK, nb)):
        top = jnp.max(gate, axis=0, keepdims=True)
        idx = jnp.min(jnp.where(gate == top, rowf, float(LANES)), axis=0, keepdims=True)
        hit = rowf == idx
        sel = jnp.where(hit & (top > -jnp.inf), 1.0, sel)
        gate = jnp.where(hit, -jnp.inf, gate)
    return jnp.where(sel > 0.0, 0.0, NEG_INF)


def _moba_tile(n_vis, nb, i, far_bias, qs, cols, k_ref, o_ref,
               kmean_scr, bias_own_scr, bias_prev_scr, vt_scr, mask_scr):
    blk, hd = MOBA_BLOCK, ATTN_HEAD_DIM
    heads = range(len(qs))
    qt = [q.astype(F32).T.astype(BF16) for q in qs]

    def rows(j):
        return pl.ds(pl.multiple_of(j * blk, blk), blk)

    n_far = n_vis - 2
    jp = jnp.maximum(i - 1, 0)
    s_own = [_dot(k_ref[rows(i), cols[g]], qt[g]) for g in heads]
    s_prev = [_dot(k_ref[rows(jp), cols[g]], qt[g]) for g in heads]

    def far_qk(j):
        return [_dot(k_ref[j * blk:(j + 1) * blk, cols[g]], qt[g]) for g in heads]
    far_s = {j: far_qk(j) for j in range(min(ATTN_QK_AHEAD, n_far))}

    m, acc = [], []
    for g in heads:
        chosen = jnp.where(i >= 1, mask_scr[g, pl.ds(jp, 1), :], NEG_INF)
        t_own = s_own[g] + bias_own_scr[g]
        t_prev = s_prev[g] + bias_prev_scr[g] + chosen
        m.append(jnp.maximum(jnp.max(t_own, axis=0, keepdims=True), jnp.max(t_prev, axis=0, keepdims=True)))
        p = jnp.concatenate([jnp.exp2(t_prev - m[g]), jnp.exp2(t_own - m[g])], axis=0).astype(BF16)
        acc.append(_dot(jnp.concatenate([vt_scr[g, jp], vt_scr[g, i]], axis=1), p))

    for j in range(n_far):
        s = far_s.pop(j)
        if j + ATTN_QK_AHEAD < n_far:
            far_s[j + ATTN_QK_AHEAD] = far_qk(j + ATTN_QK_AHEAD)
        for g in heads:
            shift = far_bias[g] + jnp.where(j < i - 1, mask_scr[g, j:j + 1, :], NEG_INF)
            m_new = jnp.maximum(m[g], jnp.max(s[g], axis=0, keepdims=True) + shift)
            p = jnp.exp2(s[g] + (shift - m_new))
            acc[g] = jnp.exp2(m[g] - m_new) * acc[g] + _dot(vt_scr[g, j], p.astype(BF16))
            m[g] = m_new

    for g in heads:
        o_ref[:, cols[g]] = (acc[g][:hd, :] / acc[g][hd:hd + 1, :]).T.astype(o_ref.dtype)


def _moba_attention(qk, v, rel_bias):
    s = qk.shape[0]
    blk, hd, nh = MOBA_BLOCK, ATTN_HEAD_DIM, ATTN_HEADS
    assert s % blk == 0 and s // blk < LANES
    assert blk >= REL_MAX_DIST
    r = jnp.arange(blk)
    dist_own = r[None, :] - r[:, None]
    bkt_own = _t5_bucket(dist_own)
    bkt_prev = _t5_bucket(dist_own + blk)
    const = lambda h, i: (0, 0)
    G = ATTN_HEADS_PER_STEP
    assert nh % G == 0
    gw = G * hd
    return pl.pallas_call(
        _moba_kernel,
        grid=(nh // G, s // blk),
        in_specs=[
            pl.BlockSpec(memory_space=pltpu.SMEM),
            pl.BlockSpec((blk, blk), const),
            pl.BlockSpec((blk, blk), const),
            pl.BlockSpec((blk, gw), lambda h, i: (i, h)),
            pl.BlockSpec((s, gw), lambda h, i: (0, nh // G + h)),
            pl.BlockSpec((s, gw), lambda h, i: (0, h)),
        ],
        out_specs=pl.BlockSpec((blk, gw), lambda h, i: (i, h)),
        out_shape=jax.ShapeDtypeStruct((s, nh * hd), BF16),
        scratch_shapes=[
            pltpu.VMEM((G, LANES, hd), F32),
            pltpu.VMEM((G, blk, blk), F32),
            pltpu.VMEM((G, blk, blk), F32),
            pltpu.VMEM((G, s // blk, hd + ATTN_SUM_ROWS, blk), BF16),
            pltpu.VMEM((G, LANES, blk), F32),
        ],
        compiler_params=_cparams(("arbitrary", "arbitrary")),
        name="moba_attention",
    )(rel_bias, bkt_own, bkt_prev, qk, qk, v)


SSD_TAIL = 8
SSD_CONV_COLS = 512


def _softplus(x):
    return jnp.maximum(x, 0.0) + jnp.log(1.0 + jnp.exp(-jnp.abs(x)))


def _split3(x):
    hi = x.astype(BF16)
    r1 = x - hi.astype(F32)
    mid = r1.astype(BF16)
    lo = (r1 - mid.astype(F32)).astype(BF16)
    return hi, mid, lo


def _ssd_kernel(xbc_ref, z_ref, dt_ref, dtt_ref, wconv_ref, bconv_ref, dtb_ref, alog_ref,
                dtbt_ref, alogt_ref, dskip_ref, norm_ref, y_ref,
                tail_scr, xact_scr, state_scr):
    c = pl.program_id(0)
    L, P, N, W = SSD_CHUNK, SSD_HEAD_DIM, SSD_STATE, SSD_WIDTH
    assert P & (P - 1) == 0

    @pl.when(c == 0)
    def _():
        tail_scr[...] = jnp.zeros(tail_scr.shape, F32)
        state_scr[...] = jnp.zeros(state_scr.shape, F32)

    rr = lax.broadcasted_iota(jnp.int32, (L, L), 0)
    cc = lax.broadcasted_iota(jnp.int32, (L, L), 1)
    causal = rr >= cc

    shifts = [jnp.where(rr - cc == k, 1.0, 0.0).astype(BF16) for k in range(1, SSD_CONV)]
    row8 = lax.broadcasted_iota(jnp.int32, (SSD_TAIL, SSD_CONV_COLS), 0)
    for c0 in range(0, SSD_XBC, SSD_CONV_COLS):
        cols = slice(c0, c0 + SSD_CONV_COLS)
        x = xbc_ref[:, cols]
        tail = tail_scr[:, cols]
        acc = bconv_ref[:, cols] + wconv_ref[SSD_CONV - 1:SSD_CONV, cols] * x.astype(F32)
        fix = jnp.zeros((SSD_TAIL, SSD_CONV_COLS), F32)
        for k in range(1, SSD_CONV):
            wk = wconv_ref[SSD_CONV - 1 - k:SSD_CONV - k, cols]
            acc = acc + wk * _dot(shifts[k - 1], x)
            fix = fix + wk * jnp.where(row8 < k, pltpu.roll(tail, k, 0), 0.0)
        acc = jnp.concatenate([acc[:SSD_TAIL] + fix, acc[SSD_TAIL:]], axis=0)
        xact_scr[:, cols] = _silu(acc)
        tail_scr[:, cols] = x[L - SSD_TAIL:, :].astype(F32)

    dtv = _softplus(dt_ref[...] + dtb_ref[...])
    ad = dtv * (-jnp.exp(alog_ref[...]))
    dtt = _softplus(dtt_ref[...] + dtbt_ref[...])
    adt = dtt * (-jnp.exp(alogt_ref[...]))
    lower = jnp.where(causal, 1.0, 0.0).astype(BF16)
    upper = jnp.where(rr <= cc, 1.0, 0.0).astype(BF16)
    a_cs = sum(_dot(lower, t) for t in _split3(ad))
    a_cst = sum(_dot(t, upper) for t in _split3(adt))
    last = a_cs[L - 1:L, :]

    factors = jnp.concatenate([dtv, jnp.exp(a_cs), jnp.exp(last - a_cs)], axis=0).astype(BF16)
    group_w = W // SSD_GROUPS
    heads_per_group = SSD_HEADS // SSD_GROUPS
    e_row = lax.broadcasted_iota(jnp.int32, (LANES, group_w), 0)
    e_head = lax.shift_right_logical(lax.broadcasted_iota(jnp.int32, (LANES, group_w), 1),
                                     P.bit_length() - 1)
    pair_lane = lax.broadcasted_iota(jnp.int32, (L, 2 * P), 1)

    for g in range(SSD_GROUPS):
        gc = slice(g * group_w, (g + 1) * group_w)
        expand = jnp.where(e_row == g * heads_per_group + e_head, 1.0, 0.0).astype(BF16)
        spread = _dot(factors, expand)
        xs = xact_scr[:, gc]
        xd = xs * spread[:L]
        xd_b = xd.astype(BF16)
        xe_b = (xd * spread[2 * L:]).astype(BF16)
        bg = xact_scr[:, W + g * N:W + (g + 1) * N].astype(BF16)
        cg = xact_scr[:, W + SSD_BC + g * N:W + SSD_BC + (g + 1) * N].astype(BF16)
        st = state_scr[gc, :]
        y_off = _dot_nt(cg, st.astype(BF16)) * spread[L:2 * L]
        new = lax.dot_general(xe_b, bg, (((0,), (0,)), ((), ())), preferred_element_type=F32)
        cb = _dot_nt(cg, bg)
        pairs = []
        for r in range(0, heads_per_group, 2):
            h = g * heads_per_group + r
            xd_pair = xd_b[:, r * P:(r + 2) * P]
            halves = []
            for hh in (h, h + 1):
                dec = jnp.exp(jnp.where(causal, a_cs[:, hh:hh + 1] - a_cst[hh:hh + 1, :], -jnp.inf))
                halves.append(_dot((cb * dec).astype(BF16), xd_pair))
                hr = slice((hh - g * heads_per_group) * P, (hh - g * heads_per_group + 1) * P)
                state_scr[g * group_w + hr.start:g * group_w + hr.stop, :] = (
                    jnp.exp(a_cst[hh:hh + 1, L - 1:L]) * st[hr, :] + new[hr, :])
            pairs.append(jnp.where(pair_lane < P, halves[0], halves[1]))
        y = jnp.concatenate(pairs, axis=1) + y_off + xs * dskip_ref[:, gc]
        y = y * _silu(z_ref[:, gc].astype(F32))
        y = y * lax.rsqrt(jnp.mean(y * y, axis=-1, keepdims=True) + NORM_EPS)
        y_ref[:, gc] = (y * norm_ref[:, gc]).astype(y_ref.dtype)


def _ssd(xbc, z, dt, w_conv, b_conv, dt_bias, a_log, d_skip, norm):
    s = xbc.shape[0]
    L, nh = SSD_CHUNK, SSD_HEADS
    assert s % L == 0
    dtt = dt[:, :nh].T
    pad = lambda v: jnp.pad(v, (0, LANES - nh)).reshape(1, LANES)
    full = lambda r, cdim: pl.BlockSpec((r, cdim), lambda c: (0, 0))
    return pl.pallas_call(
        _ssd_kernel,
        grid=(s // L,),
        in_specs=[
            pl.BlockSpec((L, SSD_XBC), lambda c: (c, 0)),
            pl.BlockSpec((L, SSD_WIDTH), lambda c: (c, 0)),
            pl.BlockSpec((L, LANES), lambda c: (c, 0)),
            pl.BlockSpec((nh, L), lambda c: (0, c)),
            full(SSD_CONV, SSD_XBC), full(1, SSD_XBC),
            full(1, LANES), full(1, LANES), full(nh, 1), full(nh, 1),
            full(1, SSD_WIDTH), full(1, SSD_WIDTH),
        ],
        out_specs=pl.BlockSpec((L, SSD_WIDTH), lambda c: (c, 0)),
        out_shape=jax.ShapeDtypeStruct((s, SSD_WIDTH), BF16),
        scratch_shapes=[
            pltpu.VMEM((SSD_TAIL, SSD_XBC), F32),
            pltpu.VMEM((L, SSD_XBC), F32),
            pltpu.VMEM((SSD_WIDTH, SSD_STATE), F32),
        ],
        compiler_params=_cparams(("arbitrary",)),
        name="ssd_scan",
    )(xbc, z, dt, dtt, w_conv, b_conv.reshape(1, SSD_XBC), pad(dt_bias), pad(a_log),
      dt_bias.reshape(nh, 1), a_log.reshape(nh, 1),
      jnp.repeat(d_skip, SSD_HEAD_DIM).reshape(1, SSD_WIDTH), norm.reshape(1, SSD_WIDTH))


HALO = 16


def _merge_kernel(ya_ref, ys_ref, cb_ref, cc_ref, cx_ref, hc_ref, hx_ref, ga_ref, gs_ref, gc_ref,
                  wsc_ref, wa_ref, ws_ref, wc_ref, o_ref, ext_scr, yc_scr):
    i = pl.program_id(0)
    j = pl.program_id(1)
    tm = ya_ref.shape[0]

    @pl.when(j == 0)
    def _():
        halo = hc_ref[...].astype(F32) * hx_ref[...].astype(F32)
        ext_scr[0:HALO, :] = jnp.where(i > 0, halo, 0.0)
        for r0 in range(0, tm, ROW_CHUNK):
            rows = slice(r0, r0 + ROW_CHUNK)
            ext_scr[HALO + r0:HALO + r0 + ROW_CHUNK, :] = cc_ref[rows, :].astype(F32) * cx_ref[rows, :].astype(F32)
        for r0 in range(0, tm, ROW_CHUNK):
            rows = slice(r0, r0 + ROW_CHUNK)
            acc = jnp.zeros((ROW_CHUNK, CONV_WIDTH), F32)
            for k in range(CONV_K):
                start = HALO + r0 - (CONV_K - 1) + k
                acc = acc + wsc_ref[k:k + 1, :] * ext_scr[start:start + ROW_CHUNK, :]
            yc_scr[rows, :] = (cb_ref[rows, :].astype(F32) * acc).astype(BF16)

    merged = (_sigmoid(ga_ref[...].astype(F32)) * _dot(ya_ref[...], wa_ref[...].astype(BF16))
              + _sigmoid(gs_ref[...].astype(F32)) * _dot(ys_ref[...], ws_ref[...].astype(BF16))
              + _sigmoid(gc_ref[...].astype(F32)) * _dot(yc_scr[...], wc_ref[...].astype(BF16)))
    o_ref[...] = merged.astype(o_ref.dtype)


def _branch_merge(y_attn, y_ssd, tail, w_sc, w_a, w_s, w_c, lead, tm=1024, tn=512):
    s = y_attn.shape[0]
    d = w_a.shape[-1]
    tm = min(tm, s)
    cw = CONV_WIDTH
    g0 = 3 * cw // tn
    gd = d // tn
    row = lambda width, cb: pl.BlockSpec((tm, width), lambda i, j: (i, cb))
    halo = lambda cb: pl.BlockSpec((HALO, cw), lambda i, j: (jnp.maximum(i * (tm // HALO) - 1, 0), cb))
    gate = lambda k: pl.BlockSpec((tm, tn), lambda i, j: (i, g0 + k * gd + j))
    wcol = lambda kdim: _wspec(lead, (kdim, tn), lambda i, j: (0, j))
    return pl.pallas_call(
        _merge_kernel,
        grid=(s // tm, d // tn),
        in_specs=[
            row(ATTN_WIDTH, 0), row(SSD_WIDTH, 0),
            row(cw, 0), row(cw, 1), row(cw, 2), halo(1), halo(2),
            gate(0), gate(1), gate(2),
            pl.BlockSpec((CONV_K, cw), lambda i, j: (0, 0)),
            wcol(ATTN_WIDTH), wcol(SSD_WIDTH), wcol(cw),
        ],
        out_specs=pl.BlockSpec((tm, tn), lambda i, j: (i, j)),
        out_shape=jax.ShapeDtypeStruct((s, d), BF16),
        scratch_shapes=[pltpu.VMEM((HALO + tm, cw), F32), pltpu.VMEM((tm, cw), BF16)],
        compiler_params=_cparams(("arbitrary", "arbitrary")),
        name="branch_merge",
    )(y_attn, y_ssd, tail, tail, tail, tail, tail, tail, tail, tail, w_sc, w_a, w_s, w_c)


def _outproj_kernel(m_ref, w_ref, x_ref, gate_ref, o_ref):
    o_ref[...] = x_ref[...] + gate_ref[...] * _dot(m_ref[...], w_ref[...].astype(BF16))


def _out_proj(merged, w, lead, x, gate, tm=1024, tn=1024):
    s, d = x.shape
    tm = min(tm, s)
    return pl.pallas_call(
        _outproj_kernel,
        grid=(d // tn, s // tm),
        in_specs=[
            pl.BlockSpec((tm, merged.shape[1]), lambda j, i: (i, 0)),
            _wspec(lead, (merged.shape[1], tn), lambda j, i: (0, j)),
            pl.BlockSpec((tm, tn), lambda j, i: (i, j)),
            pl.BlockSpec((1, tn), lambda j, i: (0, j)),
        ],
        out_specs=pl.BlockSpec((tm, tn), lambda j, i: (i, j)),
        out_shape=jax.ShapeDtypeStruct((s, d), F32),
        compiler_params=_cparams(("arbitrary", "arbitrary")),
        name="out_proj",
    )(merged, w, x, gate)


def _token_mix(x, gain, scale, shift, gate, l, w_mix_in, qk_norm, rel_bias, w_ssd_conv, b_ssd_conv,
               ssd_dt_bias, ssd_a_log, ssd_d, ssd_norm, w_sc_conv, w_br_attn, w_br_ssd, w_br_conv,
               w_mix_out):
    d = x.shape[1]
    lead = (l,)
    c_v = 2 * ATTN_WIDTH
    c_z = c_v + ATTN_WIDTH
    c_xbc = c_z + SSD_WIDTH
    c_dt = c_xbc + SSD_XBC
    c_tail = c_dt + SSD_HEADS
    n_tail = 3 * CONV_WIDTH + 3 * d
    assert w_mix_in.shape[-1] == c_tail + n_tail

    hgain = jnp.concatenate([jnp.tile(qk_norm[l, 0] * ATTN_Q_PRESCALE, ATTN_HEADS),
                             jnp.tile(qk_norm[l, 1], ATTN_HEADS)])[None]
    wt = jnp.swapaxes(w_mix_in, 1, 2)
    h = _norm(x, gain, scale, shift)
    qk = _proj(h, wt, lead, 0, 2 * ATTN_WIDTH, BF16, head_gain=hgain)
    v = _proj(h, wt, lead, c_v, ATTN_WIDTH, BF16)
    z = _proj(h, wt, lead, c_z, SSD_WIDTH, BF16)
    xbc = _proj(h, wt, lead, c_xbc, SSD_XBC, BF16)
    wt_dt = jnp.pad(wt[l, c_dt:c_tail, :], ((0, LANES - SSD_HEADS), (0, 0)))
    dt = _proj(h, wt_dt, (), 0, LANES, F32)
    tail = _proj(h, wt, lead, c_tail, n_tail, BF16)

    y_attn = _moba_attention(qk, v, rel_bias)
    y_ssd = _ssd(xbc, z, dt, w_ssd_conv[l], b_ssd_conv[l], ssd_dt_bias[l], ssd_a_log[l], ssd_d[l], ssd_norm[l])
    merged = _branch_merge(y_attn, y_ssd, tail, w_sc_conv[l], w_br_attn[l].astype(BF16),
                           w_br_ssd[l].astype(BF16), w_br_conv[l].astype(BF16), ())
    return _out_proj(merged, w_mix_out, lead, x, gate)


def kernel(x, c, w_ada, b_ada, norm_gain, w_ffn_in, w_ffn_out, w_mix_in, qk_norm, rel_bias, w_ssd_conv, b_ssd_conv, ssd_dt_bias, ssd_a_log, ssd_d, ssd_norm, w_sc_conv, w_br_attn, w_br_ssd, w_br_conv, w_mix_out):
    b, s, d = x.shape
    depth = w_ada.shape[0]
    assert b == 1 and s % math.lcm(MOBA_BLOCK, SSD_CHUNK) == 0
    xs = x.reshape(s, d)
    ada = _ada_proj(c, w_ada, b_ada).reshape(depth, N_SUBLAYERS, 3, 1, d)
    for l in range(depth):
        mod = lambda i: (norm_gain[l, i][None], ada[l, i, 1], ada[l, i, 0], ada[l, i, 2])
        xs = _ffn(xs, *mod(0), w_ffn_in, w_ffn_out, (l, 0))
        xs = _token_mix(xs, *mod(1), l, w_mix_in, qk_norm, rel_bias, w_ssd_conv, b_ssd_conv,
                        ssd_dt_bias, ssd_a_log, ssd_d, ssd_norm, w_sc_conv,
                        w_br_attn, w_br_ssd, w_br_conv, w_mix_out)
        xs = _ffn(xs, *mod(2), w_ffn_in, w_ffn_out, (l, 1))
    return xs.reshape(b, s, d)
```

```python
import functools
import math

import jax
import jax.numpy as jnp
from jax import lax
from jax.experimental import pallas as pl
from jax.experimental.pallas import tpu as pltpu

F32 = jnp.float32
BF16 = jnp.bfloat16

ATTN_HEADS = 8
ATTN_HEAD_DIM = 128
ATTN_WIDTH = ATTN_HEADS * ATTN_HEAD_DIM
MOBA_BLOCK = 256
MOBA_TOPK = 3
REL_BUCKETS = 32
REL_MAX_DIST = 128
SSD_HEADS = 32
SSD_HEAD_DIM = 64
SSD_WIDTH = SSD_HEADS * SSD_HEAD_DIM
SSD_GROUPS = 4
SSD_STATE = 128
SSD_CONV = 4
SSD_CHUNK = 256
SSD_BC = SSD_GROUPS * SSD_STATE
SSD_XBC = SSD_WIDTH + 2 * SSD_BC
CONV_WIDTH = 1024
CONV_K = 3
N_SUBLAYERS = 3
FFN_RESIDUAL = 0.5
NORM_EPS = 1e-6
NEG_INF = -1e30

LANES = 128
VMEM_LIMIT_BYTES = 58 * 1024 * 1024


def _cparams(semantics):
    return pltpu.CompilerParams(dimension_semantics=semantics, vmem_limit_bytes=VMEM_LIMIT_BYTES)


def _sigmoid(x):
    return 1.0 / (1.0 + jnp.exp(-x))


def _silu(x):
    return x * _sigmoid(x)


def _dot(a, b):
    return jnp.dot(a, b, preferred_element_type=F32)


def _dot_nt(a, b):
    return lax.dot_general(a, b, (((1,), (1,)), ((), ())), preferred_element_type=F32)


def _wspec(lead, block, index_map):
    return pl.BlockSpec((None,) * len(lead) + block, lambda i, j: lead + index_map(i, j))


def _mod_norm(x, gain, scale, shift):
    y = x * lax.rsqrt(jnp.mean(x * x, axis=-1, keepdims=True) + NORM_EPS)
    return (y * gain) * (1.0 + scale) + shift


def _ada_kernel(c_ref, w_ref, b_ref, o_ref):
    cond = _silu(c_ref[...]).astype(BF16)
    o_ref[...] = _dot(cond, w_ref[...].astype(BF16)) + b_ref[...]


def _ada_proj(c, w_ada, b_ada, tn=1024):
    depth, d, n = w_ada.shape
    c8 = jnp.broadcast_to(c, (8, d))
    out = pl.pallas_call(
        _ada_kernel,
        grid=(depth, n // tn),
        in_specs=[
            pl.BlockSpec((8, d), lambda l, j: (0, 0)),
            pl.BlockSpec((None, d, tn), lambda l, j: (l, 0, j)),
            pl.BlockSpec((None, 1, tn), lambda l, j: (l, 0, j)),
        ],
        out_specs=pl.BlockSpec((None, 8, tn), lambda l, j: (l, 0, j)),
        out_shape=jax.ShapeDtypeStruct((depth, 8, n), F32),
        compiler_params=_cparams(("arbitrary", "arbitrary")),
        name="ada_proj",
    )(c8, w_ada, b_ada.reshape(depth, 1, n))
    return out[:, 0, :]


ROW_CHUNK = 128
NORM_ROWS = 128
COL_CHUNK = 512


def _ffn_kernel(x_ref, gain_ref, scale_ref, shift_ref, gate_ref, wg_ref, wu_ref, wo_ref,
                o_ref, h_scr):
    j = pl.program_id(1)
    tm, d = x_ref.shape

    @pl.when(j == 0)
    def _():
        def body(r, carry):
            rows = pl.ds(pl.multiple_of(r * NORM_ROWS, NORM_ROWS), NORM_ROWS)
            h = _mod_norm(x_ref[rows, :], gain_ref[...], scale_ref[...], shift_ref[...])
            h_scr[rows, :] = h.astype(BF16)
            o_ref[rows, :] = jnp.zeros((NORM_ROWS, d), F32)
            return carry
        lax.fori_loop(0, tm // NORM_ROWS, body, 0)

    h = h_scr[...]
    g = _dot(h, wg_ref[...].astype(BF16))
    u = _dot(h, wu_ref[...].astype(BF16))
    a = (_silu(g) * u).astype(BF16)
    for c in range(0, d, COL_CHUNK):
        o_ref[:, c:c + COL_CHUNK] += _dot(a, wo_ref[:, c:c + COL_CHUNK].astype(BF16))

    @pl.when(j == pl.num_programs(1) - 1)
    def _():
        def body(r, carry):
            rows = pl.ds(pl.multiple_of(r * ROW_CHUNK, ROW_CHUNK), ROW_CHUNK)
            o_ref[rows, :] = x_ref[rows, :] + (FFN_RESIDUAL * gate_ref[...]) * o_ref[rows, :]
            return carry
        lax.fori_loop(0, tm // ROW_CHUNK, body, 0)


def _ffn(x, gain, scale, shift, gate, w_in, w_out, lead, tm=1024, tf=256):
    s, d = x.shape
    f = w_out.shape[-2]
    tm = min(tm, s)
    nf = f // tf
    vec = pl.BlockSpec((1, d), lambda i, j: (0, 0))
    return pl.pallas_call(
        _ffn_kernel,
        grid=(s // tm, nf),
        in_specs=[
            pl.BlockSpec((tm, d), lambda i, j: (i, 0)),
            vec, vec, vec, vec,
            _wspec(lead, (d, tf), lambda i, j: (0, j)),
            _wspec(lead, (d, tf), lambda i, j: (0, j + nf)),
            _wspec(lead, (tf, d), lambda i, j: (j, 0)),
        ],
        out_specs=pl.BlockSpec((tm, d), lambda i, j: (i, 0)),
        out_shape=jax.ShapeDtypeStruct((s, d), F32),
        scratch_shapes=[pltpu.VMEM((tm, d), BF16)],
        compiler_params=_cparams(("arbitrary", "arbitrary")),
        name="ffn",
    )(x, gain, scale, shift, gate, w_in, w_in, w_out)


def _head_rms_norm(y, gain):
    outs = []
    for c in range(0, y.shape[1], ATTN_HEAD_DIM):
        yc = y[:, c:c + ATTN_HEAD_DIM]
        outs.append(yc * lax.rsqrt(jnp.mean(yc * yc, axis=-1, keepdims=True) + NORM_EPS))
    return jnp.concatenate(outs, axis=1) * gain


def _norm_kernel(x_ref, gain_ref, scale_ref, shift_ref, h_ref):
    def body(r, carry):
        rows = pl.ds(pl.multiple_of(r * NORM_ROWS, NORM_ROWS), NORM_ROWS)
        h = _mod_norm(x_ref[rows, :], gain_ref[...], scale_ref[...], shift_ref[...])
        h_ref[rows, :] = h.astype(BF16)
        return carry
    lax.fori_loop(0, x_ref.shape[0] // NORM_ROWS, body, 0)


def _norm(x, gain, scale, shift, tm=512):
    s, d = x.shape
    tm = min(tm, s)
    vec = pl.BlockSpec((1, d), lambda i: (0, 0))
    return pl.pallas_call(
        _norm_kernel,
        grid=(s // tm,),
        in_specs=[pl.BlockSpec((tm, d), lambda i: (i, 0)), vec, vec, vec],
        out_specs=pl.BlockSpec((tm, d), lambda i: (i, 0)),
        out_shape=jax.ShapeDtypeStruct((s, d), BF16),
        compiler_params=_cparams(("arbitrary",)),
        name="mod_norm",
    )(x, gain, scale, shift)


def _proj_kernel(h_ref, *refs, shift, head_norm, transpose_out):
    wa_ref, o_ref, w_scr = refs[0], refs[-2], refs[-1]
    tn = wa_ref.shape[0]

    @pl.when(pl.program_id(1) == 0)
    def _():
        w_scr[:tn - shift, :] = wa_ref[shift:, :].astype(BF16)
        if shift:
            w_scr[tn - shift:, :] = refs[1][:shift, :].astype(BF16)

    if transpose_out:
        o_ref[...] = _dot_nt(w_scr[...], h_ref[...]).astype(o_ref.dtype)
        return
    y = _dot_nt(h_ref[...], w_scr[...])
    if head_norm:
        y = _head_rms_norm(y, refs[-3][...])
    o_ref[...] = y.astype(o_ref.dtype)


PROJ_SHIFT_ROWS = 128


def _proj(h, wt, lead, row0, n, out_dtype, head_gain=None, transpose_out=False, tm=1024, tn=1024):
    s, d = h.shape
    tm = min(tm, s)
    tn = min(tn, n)
    base = row0 // tn * tn
    shift = row0 - base
    assert n % tn == 0 and shift % 16 == 0 and shift <= PROJ_SHIFT_ROWS and tn % PROJ_SHIFT_ROWS == 0
    specs = [pl.BlockSpec((tm, d), lambda j, i: (i, 0)),
             _wspec(lead, (tn, d), lambda j, i: (base // tn + j, 0))]
    args = [h, wt]
    if shift:
        specs.append(_wspec(lead, (PROJ_SHIFT_ROWS, d),
                            lambda j, i: ((base + (j + 1) * tn) // PROJ_SHIFT_ROWS, 0)))
        args.append(wt)
    if head_gain is not None:
        specs.append(pl.BlockSpec((1, tn), lambda j, i: (0, j)))
        args.append(head_gain)
    assert not (transpose_out and head_gain is not None)
    if transpose_out:
        out_spec, out_shape = pl.BlockSpec((tn, tm), lambda j, i: (j, i)), (n, s)
    else:
        out_spec, out_shape = pl.BlockSpec((tm, tn), lambda j, i: (i, j)), (s, n)
    return pl.pallas_call(
        functools.partial(_proj_kernel, shift=shift, head_norm=head_gain is not None,
                          transpose_out=transpose_out),
        grid=(n // tn, s // tm),
        in_specs=specs,
        out_specs=out_spec,
        out_shape=jax.ShapeDtypeStruct(out_shape, out_dtype),
        scratch_shapes=[pltpu.VMEM((tn, d), BF16)],
        compiler_params=_cparams(("arbitrary", "arbitrary")),
        name="proj",
    )(*args)


def _t5_bucket(dist):
    n = jnp.maximum(dist, 0)
    max_exact = REL_BUCKETS // 2
    ratio = jnp.log(jnp.maximum(n, 1).astype(F32) / max_exact) / math.log(REL_MAX_DIST / max_exact)
    large = max_exact + (ratio * (REL_BUCKETS - max_exact)).astype(jnp.int32)
    large = jnp.minimum(large, REL_BUCKETS - 1)
    return jnp.where(n < max_exact, n, large)


ATTN_VIS_STEPS = tuple(range(2, 33, 2))
ATTN_QK_AHEAD = 2
ATTN_HEADS_PER_STEP = 2
ATTN_SUM_ROWS = 16
LOG2E = 1.4426950408889634
ATTN_Q_PRESCALE = ATTN_HEAD_DIM ** -0.5 * LOG2E


def _moba_kernel(rel_ref, bkt_own_ref, bkt_prev_ref, q_ref, k_ref, vt_ref, o_ref,
                 kmean_scr, bias_own_scr, bias_prev_scr, vt_scr, mask_scr):
    hg = pl.program_id(0)
    i = pl.program_id(1)
    blk, hd, G = MOBA_BLOCK, ATTN_HEAD_DIM, ATTN_HEADS_PER_STEP
    nb = k_ref.shape[0] // blk
    heads = [hg * G + g for g in range(G)]
    cols = [slice(g * hd, (g + 1) * hd) for g in range(G)]

    @pl.when(i == 0)
    def _():
        s_len = k_ref.shape[0]
        avg = jnp.where(lax.shift_right_logical(lax.broadcasted_iota(jnp.int32, (LANES, s_len), 1),
                                                blk.bit_length() - 1)
                        == lax.broadcasted_iota(jnp.int32, (LANES, s_len), 0), 1.0 / blk, 0.0).astype(BF16)
        ones_row = jnp.where(lax.broadcasted_iota(jnp.int32, (ATTN_SUM_ROWS, blk), 0) == 0, 1.0, 0.0)
        key = lax.broadcasted_iota(jnp.int32, (blk, blk), 0)
        qry = lax.broadcasted_iota(jnp.int32, (blk, blk), 1)
        for g in range(G):
            kmean_scr[g] = _dot(avg, k_ref[:, cols[g]])
            for b in range(nb):
                vt_scr[g, b, :hd, :] = vt_ref[g * hd:(g + 1) * hd, b * blk:(b + 1) * blk]
                vt_scr[g, b, hd:, :] = ones_row.astype(BF16)

            bias_own_scr[g] = jnp.where(key <= qry, 0.0, NEG_INF)
            bias_prev_scr[g] = jnp.zeros((blk, blk), F32)

            def bias_body(b, carry, g=g):
                val = rel_ref[b, heads[g]] * LOG2E
                bias_own_scr[g] += jnp.where(bkt_own_ref[...] == b, val, 0.0)
                bias_prev_scr[g] += jnp.where(bkt_prev_ref[...] == b, val, 0.0)
                return carry
            lax.fori_loop(0, REL_BUCKETS, bias_body, 0)

    qs = [q_ref[:, cols[g]] for g in range(G)]
    for g in range(G):
        mask_scr[g] = _choose_blocks(i, nb, qs[g], kmean_scr[g])

    far_bias = [rel_ref[REL_BUCKETS - 1, heads[g]] * LOG2E for g in range(G)]
    visible = i + 1
    lo = 0
    for n_vis in sorted({min(v, nb) for v in ATTN_VIS_STEPS} | {nb}):
        @pl.when((visible > lo) & (visible <= n_vis))
        def _(n_vis=n_vis):
            _moba_tile(n_vis, nb, i, far_bias, qs, cols, k_ref, o_ref,
                       kmean_scr, bias_own_scr, bias_prev_scr, vt_scr, mask_scr)
        lo = n_vis


def _choose_blocks(i, nb, q, kmean):
    km_hi = kmean.astype(BF16)
    km_lo = (kmean - km_hi.astype(F32)).astype(BF16)
    gate = _dot_nt(km_hi, q) + _dot_nt(km_lo, q)
    rowb = lax.broadcasted_iota(jnp.int32, gate.shape, 0)
    rowf = rowb.astype(F32)
    gate = jnp.where(rowb < i, gate, -jnp.inf)
    sel = jnp.zeros(gate.shape, F32)
    for _ in range(min(MOBA_TOPK, nb)):
        top = jnp.max(gate, axis=0, keepdims=True)
        idx = jnp.min(jnp.where(gate == top, rowf, float(LANES)), axis=0, keepdims=True)
        hit = rowf == idx
        sel = jnp.where(hit & (top > -jnp.inf), 1.0, sel)
        gate = jnp.where(hit, -jnp.inf, gate)
    return jnp.where(sel > 0.0, 0.0, NEG_INF)


def _moba_tile(n_vis, nb, i, far_bias, qs, cols, k_ref, o_ref,
               kmean_scr, bias_own_scr, bias_prev_scr, vt_scr, mask_scr):
    blk, hd = MOBA_BLOCK, ATTN_HEAD_DIM
    heads = range(len(qs))
    qt = [q.astype(F32).T.astype(BF16) for q in qs]

    def rows(j):
        return pl.ds(pl.multiple_of(j * blk, blk), blk)

    n_far = n_vis - 2
    jp = jnp.maximum(i - 1, 0)
    s_own = [_dot(k_ref[rows(i), cols[g]], qt[g]) for g in heads]
    s_prev = [_dot(k_ref[rows(jp), cols[g]], qt[g]) for g in heads]

    def far_qk(j):
        return [_dot(k_ref[j * blk:(j + 1) * blk, cols[g]], qt[g]) for g in heads]
    far_s = {j: far_qk(j) for j in range(min(ATTN_QK_AHEAD, n_far))}

    m, acc = [], []
    for g in heads:
        chosen = jnp.where(i >= 1, mask_scr[g, pl.ds(jp, 1), :], NEG_INF)
        t_own = s_own[g] + bias_own_scr[g]
        t_prev = s_prev[g] + bias_prev_scr[g] + chosen
        m.append(jnp.maximum(jnp.max(t_own, axis=0, keepdims=True), jnp.max(t_prev, axis=0, keepdims=True)))
        p = jnp.concatenate([jnp.exp2(t_prev - m[g]), jnp.exp2(t_own - m[g])], axis=0).astype(BF16)
        acc.append(_dot(jnp.concatenate([vt_scr[g, jp], vt_scr[g, i]], axis=1), p))

    for j in range(n_far):
        s = far_s.pop(j)
        if j + ATTN_QK_AHEAD < n_far:
            far_s[j + ATTN_QK_AHEAD] = far_qk(j + ATTN_QK_AHEAD)
        for g in heads:
            shift = far_bias[g] + jnp.where(j < i - 1, mask_scr[g, j:j + 1, :], NEG_INF)
            m_new = jnp.maximum(m[g], jnp.max(s[g], axis=0, keepdims=True) + shift)
            p = jnp.exp2(s[g] + (shift - m_new))
            acc[g] = jnp.exp2(m[g] - m_new) * acc[g] + _dot(vt_scr[g, j], p.astype(BF16))
            m[g] = m_new

    for g in heads:
        o_ref[:, cols[g]] = (acc[g][:hd, :] / acc[g][hd:hd + 1, :]).T.astype(o_ref.dtype)


def _moba_attention(qk, vt, rel_bias):
    s = qk.shape[0]
    blk, hd, nh = MOBA_BLOCK, ATTN_HEAD_DIM, ATTN_HEADS
    assert s % blk == 0 and s // blk < LANES
    assert blk & (blk - 1) == 0
    assert blk >= REL_MAX_DIST
    r = jnp.arange(blk)
    dist_own = r[None, :] - r[:, None]
    bkt_own = _t5_bucket(dist_own)
    bkt_prev = _t5_bucket(dist_own + blk)
    const = lambda h, i: (0, 0)
    G = ATTN_HEADS_PER_STEP
    assert nh % G == 0
    gw = G * hd
    return pl.pallas_call(
        _moba_kernel,
        grid=(nh // G, s // blk),
        in_specs=[
            pl.BlockSpec(memory_space=pltpu.SMEM),
            pl.BlockSpec((blk, blk), const),
            pl.BlockSpec((blk, blk), const),
            pl.BlockSpec((blk, gw), lambda h, i: (i, h)),
            pl.BlockSpec((s, gw), lambda h, i: (0, nh // G + h)),
            pl.BlockSpec((gw, s), lambda h, i: (h, 0)),
        ],
        out_specs=pl.BlockSpec((blk, gw), lambda h, i: (i, h)),
        out_shape=jax.ShapeDtypeStruct((s, nh * hd), BF16),
        scratch_shapes=[
            pltpu.VMEM((G, LANES, hd), F32),
            pltpu.VMEM((G, blk, blk), F32),
            pltpu.VMEM((G, blk, blk), F32),
            pltpu.VMEM((G, s // blk, hd + ATTN_SUM_ROWS, blk), BF16),
            pltpu.VMEM((G, LANES, blk), F32),
        ],
        compiler_params=_cparams(("arbitrary", "arbitrary")),
        name="moba_attention",
    )(rel_bias, bkt_own, bkt_prev, qk, qk, vt)


SSD_TAIL = 8
SSD_CONV_COLS = 512


def _softplus(x):
    return jnp.maximum(x, 0.0) + jnp.log(1.0 + jnp.exp(-jnp.abs(x)))


def _split3(x):
    hi = x.astype(BF16)
    r1 = x - hi.astype(F32)
    mid = r1.astype(BF16)
    lo = (r1 - mid.astype(F32)).astype(BF16)
    return hi, mid, lo


def _ssd_kernel(xbc_ref, z_ref, dt_ref, dtt_ref, wconv_ref, bconv_ref, dtb_ref, alog_ref,
                dtbt_ref, alogt_ref, dskip_ref, norm_ref, y_ref,
                tail_scr, xact_scr, state_scr):
    c = pl.program_id(0)
    L, P, N, W = SSD_CHUNK, SSD_HEAD_DIM, SSD_STATE, SSD_WIDTH
    assert P & (P - 1) == 0

    @pl.when(c == 0)
    def _():
        tail_scr[...] = jnp.zeros(tail_scr.shape, F32)
        state_scr[...] = jnp.zeros(state_scr.shape, F32)

    rr = lax.broadcasted_iota(jnp.int32, (L, L), 0)
    cc = lax.broadcasted_iota(jnp.int32, (L, L), 1)
    causal = rr >= cc

    shifts = [jnp.where(rr - cc == k, 1.0, 0.0).astype(BF16) for k in range(1, SSD_CONV)]
    row8 = lax.broadcasted_iota(jnp.int32, (SSD_TAIL, SSD_CONV_COLS), 0)
    for c0 in range(0, SSD_XBC, SSD_CONV_COLS):
        cols = slice(c0, c0 + SSD_CONV_COLS)
        x = xbc_ref[:, cols]
        tail = tail_scr[:, cols]
        acc = bconv_ref[:, cols] + wconv_ref[SSD_CONV - 1:SSD_CONV, cols] * x.astype(F32)
        fix = jnp.zeros((SSD_TAIL, SSD_CONV_COLS), F32)
        for k in range(1, SSD_CONV):
            wk = wconv_ref[SSD_CONV - 1 - k:SSD_CONV - k, cols]
            acc = acc + wk * _dot(shifts[k - 1], x)
            fix = fix + wk * jnp.where(row8 < k, pltpu.roll(tail, k, 0), 0.0)
        acc = jnp.concatenate([acc[:SSD_TAIL] + fix, acc[SSD_TAIL:]], axis=0)
        xact_scr[:, cols] = _silu(acc)
        tail_scr[:, cols] = x[L - SSD_TAIL:, :].astype(F32)

    dtv = _softplus(dt_ref[...] + dtb_ref[...])
    ad = dtv * (-jnp.exp(alog_ref[...]))
    dtt = _softplus(dtt_ref[...] + dtbt_ref[...])
    adt = dtt * (-jnp.exp(alogt_ref[...]))
    lower = jnp.where(causal, 1.0, 0.0).astype(BF16)
    upper = jnp.where(rr <= cc, 1.0, 0.0).astype(BF16)
    a_cs = sum(_dot(lower, t) for t in _split3(ad))
    a_cst = sum(_dot(t, upper) for t in _split3(adt))
    last = a_cs[L - 1:L, :]

    factors = jnp.concatenate([dtv, jnp.exp(a_cs), jnp.exp(last - a_cs)], axis=0).astype(BF16)
    group_w = W // SSD_GROUPS
    heads_per_group = SSD_HEADS // SSD_GROUPS
    e_row = lax.broadcasted_iota(jnp.int32, (LANES, group_w), 0)
    e_head = lax.shift_right_logical(lax.broadcasted_iota(jnp.int32, (LANES, group_w), 1),
                                     P.bit_length() - 1)
    pair_lane = lax.broadcasted_iota(jnp.int32, (L, 2 * P), 1)

    for g in range(SSD_GROUPS):
        gc = slice(g * group_w, (g + 1) * group_w)
        expand = jnp.where(e_row == g * heads_per_group + e_head, 1.0, 0.0).astype(BF16)
        spread = _dot(factors, expand)
        xs = xact_scr[:, gc]
        xd = xs * spread[:L]
        xd_b = xd.astype(BF16)
        xe_b = (xd * spread[2 * L:]).astype(BF16)
        bg = xact_scr[:, W + g * N:W + (g + 1) * N].astype(BF16)
        cg = xact_scr[:, W + SSD_BC + g * N:W + SSD_BC + (g + 1) * N].astype(BF16)
        st = state_scr[gc, :]
        y_off = _dot_nt(cg, st.astype(BF16)) * spread[L:2 * L]
        new = lax.dot_general(xe_b, bg, (((0,), (0,)), ((), ())), preferred_element_type=F32)
        cb = _dot_nt(cg, bg)
        pairs = []
        for r in range(0, heads_per_group, 2):
            h = g * heads_per_group + r
            xd_pair = xd_b[:, r * P:(r + 2) * P]
            halves = []
            for hh in (h, h + 1):
                dec = jnp.exp(jnp.where(causal, a_cs[:, hh:hh + 1] - a_cst[hh:hh + 1, :], -jnp.inf))
                halves.append(_dot((cb * dec).astype(BF16), xd_pair))
                hr = slice((hh - g * heads_per_group) * P, (hh - g * heads_per_group + 1) * P)
                state_scr[g * group_w + hr.start:g * group_w + hr.stop, :] = (
                    jnp.exp(a_cst[hh:hh + 1, L - 1:L]) * st[hr, :] + new[hr, :])
            pairs.append(jnp.where(pair_lane < P, halves[0], halves[1]))
        y = jnp.concatenate(pairs, axis=1) + y_off + xs * dskip_ref[:, gc]
        y = y * _silu(z_ref[:, gc].astype(F32))
        y = y * lax.rsqrt(jnp.mean(y * y, axis=-1, keepdims=True) + NORM_EPS)
        y_ref[:, gc] = (y * norm_ref[:, gc]).astype(y_ref.dtype)


def _ssd(xbc, z, dt, w_conv, b_conv, dt_bias, a_log, d_skip, norm):
    s = xbc.shape[0]
    L, nh = SSD_CHUNK, SSD_HEADS
    assert s % L == 0
    dtt = dt[:, :nh].T
    pad = lambda v: jnp.pad(v, (0, LANES - nh)).reshape(1, LANES)
    full = lambda r, cdim: pl.BlockSpec((r, cdim), lambda c: (0, 0))
    return pl.pallas_call(
        _ssd_kernel,
        grid=(s // L,),
        in_specs=[
            pl.BlockSpec((L, SSD_XBC), lambda c: (c, 0)),
            pl.BlockSpec((L, SSD_WIDTH), lambda c: (c, 0)),
            pl.BlockSpec((L, LANES), lambda c: (c, 0)),
            pl.BlockSpec((nh, L), lambda c: (0, c)),
            full(SSD_CONV, SSD_XBC), full(1, SSD_XBC),
            full(1, LANES), full(1, LANES), full(nh, 1), full(nh, 1),
            full(1, SSD_WIDTH), full(1, SSD_WIDTH),
        ],
        out_specs=pl.BlockSpec((L, SSD_WIDTH), lambda c: (c, 0)),
        out_shape=jax.ShapeDtypeStruct((s, SSD_WIDTH), BF16),
        scratch_shapes=[
            pltpu.VMEM((SSD_TAIL, SSD_XBC), F32),
            pltpu.VMEM((L, SSD_XBC), F32),
            pltpu.VMEM((SSD_WIDTH, SSD_STATE), F32),
        ],
        compiler_params=_cparams(("arbitrary",)),
        name="ssd_scan",
    )(xbc, z, dt, dtt, w_conv, b_conv.reshape(1, SSD_XBC), pad(dt_bias), pad(a_log),
      dt_bias.reshape(nh, 1), a_log.reshape(nh, 1),
      jnp.repeat(d_skip, SSD_HEAD_DIM).reshape(1, SSD_WIDTH), norm.reshape(1, SSD_WIDTH))


HALO = 16


def _merge_kernel(ya_ref, ys_ref, cb_ref, cc_ref, cx_ref, hc_ref, hx_ref, ga_ref, gs_ref, gc_ref,
                  wsc_ref, wa_ref, ws_ref, wc_ref, o_ref, ext_scr, yc_scr):
    i = pl.program_id(0)
    j = pl.program_id(1)
    tm = ya_ref.shape[0]

    @pl.when(j == 0)
    def _():
        halo = hc_ref[...].astype(F32) * hx_ref[...].astype(F32)
        ext_scr[0:HALO, :] = jnp.where(i > 0, halo, 0.0)
        for r0 in range(0, tm, ROW_CHUNK):
            rows = slice(r0, r0 + ROW_CHUNK)
            ext_scr[HALO + r0:HALO + r0 + ROW_CHUNK, :] = cc_ref[rows, :].astype(F32) * cx_ref[rows, :].astype(F32)
        for r0 in range(0, tm, ROW_CHUNK):
            rows = slice(r0, r0 + ROW_CHUNK)
            acc = jnp.zeros((ROW_CHUNK, CONV_WIDTH), F32)
            for k in range(CONV_K):
                start = HALO + r0 - (CONV_K - 1) + k
                acc = acc + wsc_ref[k:k + 1, :] * ext_scr[start:start + ROW_CHUNK, :]
            yc_scr[rows, :] = (cb_ref[rows, :].astype(F32) * acc).astype(BF16)

    merged = (_sigmoid(ga_ref[...].astype(F32)) * _dot(ya_ref[...], wa_ref[...].astype(BF16))
              + _sigmoid(gs_ref[...].astype(F32)) * _dot(ys_ref[...], ws_ref[...].astype(BF16))
              + _sigmoid(gc_ref[...].astype(F32)) * _dot(yc_scr[...], wc_ref[...].astype(BF16)))
    o_ref[...] = merged.astype(o_ref.dtype)


def _branch_merge(y_attn, y_ssd, tail, w_sc, w_a, w_s, w_c, lead, tm=1024, tn=512):
    s = y_attn.shape[0]
    d = w_a.shape[-1]
    tm = min(tm, s)
    cw = CONV_WIDTH
    g0 = 3 * cw // tn
    gd = d // tn
    row = lambda width, cb: pl.BlockSpec((tm, width), lambda i, j: (i, cb))
    halo = lambda cb: pl.BlockSpec((HALO, cw), lambda i, j: (jnp.maximum(i * (tm // HALO) - 1, 0), cb))
    gate = lambda k: pl.BlockSpec((tm, tn), lambda i, j: (i, g0 + k * gd + j))
    wcol = lambda kdim: _wspec(lead, (kdim, tn), lambda i, j: (0, j))
    return pl.pallas_call(
        _merge_kernel,
        grid=(s // tm, d // tn),
        in_specs=[
            row(ATTN_WIDTH, 0), row(SSD_WIDTH, 0),
            row(cw, 0), row(cw, 1), row(cw, 2), halo(1), halo(2),
            gate(0), gate(1), gate(2),
            pl.BlockSpec((CONV_K, cw), lambda i, j: (0, 0)),
            wcol(ATTN_WIDTH), wcol(SSD_WIDTH), wcol(cw),
        ],
        out_specs=pl.BlockSpec((tm, tn), lambda i, j: (i, j)),
        out_shape=jax.ShapeDtypeStruct((s, d), BF16),
        scratch_shapes=[pltpu.VMEM((HALO + tm, cw), F32), pltpu.VMEM((tm, cw), BF16)],
        compiler_params=_cparams(("arbitrary", "arbitrary")),
        name="branch_merge",
    )(y_attn, y_ssd, tail, tail, tail, tail, tail, tail, tail, tail, w_sc, w_a, w_s, w_c)


def _outproj_kernel(m_ref, w_ref, x_ref, gate_ref, o_ref):
    o_ref[...] = x_ref[...] + gate_ref[...] * _dot(m_ref[...], w_ref[...].astype(BF16))


def _out_proj(merged, w, lead, x, gate, tm=1024, tn=1024):
    s, d = x.shape
    tm = min(tm, s)
    return pl.pallas_call(
        _outproj_kernel,
        grid=(d // tn, s // tm),
        in_specs=[
            pl.BlockSpec((tm, merged.shape[1]), lambda j, i: (i, 0)),
            _wspec(lead, (merged.shape[1], tn), lambda j, i: (0, j)),
            pl.BlockSpec((tm, tn), lambda j, i: (i, j)),
            pl.BlockSpec((1, tn), lambda j, i: (0, j)),
        ],
        out_specs=pl.BlockSpec((tm, tn), lambda j, i: (i, j)),
        out_shape=jax.ShapeDtypeStruct((s, d), F32),
        compiler_params=_cparams(("arbitrary", "arbitrary")),
        name="out_proj",
    )(merged, w, x, gate)


def _token_mix(x, gain, scale, shift, gate, l, w_mix_in, qk_norm, rel_bias, w_ssd_conv, b_ssd_conv,
               ssd_dt_bias, ssd_a_log, ssd_d, ssd_norm, w_sc_conv, w_br_attn, w_br_ssd, w_br_conv,
               w_mix_out):
    d = x.shape[1]
    lead = (l,)
    c_v = 2 * ATTN_WIDTH
    c_z = c_v + ATTN_WIDTH
    c_xbc = c_z + SSD_WIDTH
    c_dt = c_xbc + SSD_XBC
    c_tail = c_dt + SSD_HEADS
    n_tail = 3 * CONV_WIDTH + 3 * d
    assert w_mix_in.shape[-1] == c_tail + n_tail

    hgain = jnp.concatenate([jnp.tile(qk_norm[l, 0] * ATTN_Q_PRESCALE, ATTN_HEADS),
                             jnp.tile(qk_norm[l, 1], ATTN_HEADS)])[None]
    wt = jnp.swapaxes(w_mix_in, 1, 2)
    h = _norm(x, gain, scale, shift)
    qk = _proj(h, wt, lead, 0, 2 * ATTN_WIDTH, BF16, head_gain=hgain)
    vt = _proj(h, wt, lead, c_v, ATTN_WIDTH, BF16, transpose_out=True)
    z = _proj(h, wt, lead, c_z, SSD_WIDTH, BF16)
    xbc = _proj(h, wt, lead, c_xbc, SSD_XBC, BF16)
    wt_dt = jnp.pad(wt[l, c_dt:c_tail, :], ((0, LANES - SSD_HEADS), (0, 0)))
    dt = _proj(h, wt_dt, (), 0, LANES, F32)
    tail = _proj(h, wt, lead, c_tail, n_tail, BF16)

    y_attn = _moba_attention(qk, vt, rel_bias)
    y_ssd = _ssd(xbc, z, dt, w_ssd_conv[l], b_ssd_conv[l], ssd_dt_bias[l], ssd_a_log[l], ssd_d[l], ssd_norm[l])
    merged = _branch_merge(y_attn, y_ssd, tail, w_sc_conv[l], w_br_attn[l].astype(BF16),
                           w_br_ssd[l].astype(BF16), w_br_conv[l].astype(BF16), ())
    return _out_proj(merged, w_mix_out, lead, x, gate)


def kernel(x, c, w_ada, b_ada, norm_gain, w_ffn_in, w_ffn_out, w_mix_in, qk_norm, rel_bias, w_ssd_conv, b_ssd_conv, ssd_dt_bias, ssd_a_log, ssd_d, ssd_norm, w_sc_conv, w_br_attn, w_br_ssd, w_br_conv, w_mix_out):
    b, s, d = x.shape
    depth = w_ada.shape[0]
    assert b == 1 and s % math.lcm(MOBA_BLOCK, SSD_CHUNK) == 0
    xs = x.reshape(s, d)
    ada = _ada_proj(c, w_ada, b_ada).reshape(depth, N_SUBLAYERS, 3, 1, d)
    for l in range(depth):
        mod = lambda i: (norm_gain[l, i][None], ada[l, i, 1], ada[l, i, 0], ada[l, i, 2])
        xs = _ffn(xs, *mod(0), w_ffn_in, w_ffn_out, (l, 0))
        xs = _token_mix(xs, *mod(1), l, w_mix_in, qk_norm, rel_bias, w_ssd_conv, b_ssd_conv,
                        ssd_dt_bias, ssd_a_log, ssd_d, ssd_norm, w_sc_conv,
                        w_br_attn, w_br_ssd, w_br_conv, w_mix_out)
        xs = _ffn(xs, *mod(2), w_ffn_in, w_ffn_out, (l, 1))
    return xs.reshape(b, s, d)
```

```python
import functools
import math

import jax
import jax.numpy as jnp
from jax import lax
from jax.experimental import pallas as pl
from jax.experimental.pallas import tpu as pltpu

F32 = jnp.float32
BF16 = jnp.bfloat16

ATTN_HEADS = 8
ATTN_HEAD_DIM = 128
ATTN_WIDTH = ATTN_HEADS * ATTN_HEAD_DIM
MOBA_BLOCK = 256
MOBA_TOPK = 3
REL_BUCKETS = 32
REL_MAX_DIST = 128
SSD_HEADS = 32
SSD_HEAD_DIM = 64
SSD_WIDTH = SSD_HEADS * SSD_HEAD_DIM
SSD_GROUPS = 4
SSD_STATE = 128
SSD_CONV = 4
SSD_CHUNK = 256
SSD_BC = SSD_GROUPS * SSD_STATE
SSD_XBC = SSD_WIDTH + 2 * SSD_BC
CONV_WIDTH = 1024
CONV_K = 3
N_SUBLAYERS = 3
FFN_RESIDUAL = 0.5
NORM_EPS = 1e-6
NEG_INF = -1e30

LANES = 128
VMEM_LIMIT_BYTES = 58 * 1024 * 1024


def _cparams(semantics):
    return pltpu.CompilerParams(dimension_semantics=semantics, vmem_limit_bytes=VMEM_LIMIT_BYTES)


def _sigmoid(x):
    return 1.0 / (1.0 + jnp.exp(-x))


def _silu(x):
    return x * _sigmoid(x)


def _dot(a, b):
    return jnp.dot(a, b, preferred_element_type=F32)


def _dot_nt(a, b):
    return lax.dot_general(a, b, (((1,), (1,)), ((), ())), preferred_element_type=F32)


def _wspec(lead, block, index_map):
    return pl.BlockSpec((None,) * len(lead) + block, lambda i, j: lead + index_map(i, j))


def _mod_norm(x, gain, scale, shift):
    y = x * lax.rsqrt(jnp.mean(x * x, axis=-1, keepdims=True) + NORM_EPS)
    return (y * gain) * (1.0 + scale) + shift


def _ada_kernel(c_ref, w_ref, b_ref, o_ref):
    cond = _silu(c_ref[...]).astype(BF16)
    o_ref[...] = _dot(cond, w_ref[...].astype(BF16)) + b_ref[...]


def _ada_proj(c, w_ada, b_ada, tn=1024):
    depth, d, n = w_ada.shape
    c8 = jnp.broadcast_to(c, (8, d))
    out = pl.pallas_call(
        _ada_kernel,
        grid=(depth, n // tn),
        in_specs=[
            pl.BlockSpec((8, d), lambda l, j: (0, 0)),
            pl.BlockSpec((None, d, tn), lambda l, j: (l, 0, j)),
            pl.BlockSpec((None, 1, tn), lambda l, j: (l, 0, j)),
        ],
        out_specs=pl.BlockSpec((None, 8, tn), lambda l, j: (l, 0, j)),
        out_shape=jax.ShapeDtypeStruct((depth, 8, n), F32),
        compiler_params=_cparams(("arbitrary", "arbitrary")),
        name="ada_proj",
    )(c8, w_ada, b_ada.reshape(depth, 1, n))
    return out[:, 0, :]


ROW_CHUNK = 128
NORM_ROWS = 128
COL_CHUNK = 512


def _ffn_kernel(x_ref, gain_ref, scale_ref, shift_ref, gate_ref, wg_ref, wu_ref, wo_ref,
                o_ref, h_scr):
    j = pl.program_id(1)
    tm, d = x_ref.shape

    @pl.when(j == 0)
    def _():
        def body(r, carry):
            rows = pl.ds(pl.multiple_of(r * NORM_ROWS, NORM_ROWS), NORM_ROWS)
            h = _mod_norm(x_ref[rows, :], gain_ref[...], scale_ref[...], shift_ref[...])
            h_scr[rows, :] = h.astype(BF16)
            o_ref[rows, :] = jnp.zeros((NORM_ROWS, d), F32)
            return carry
        lax.fori_loop(0, tm // NORM_ROWS, body, 0)

    h = h_scr[...]
    g = _dot(h, wg_ref[...].astype(BF16))
    u = _dot(h, wu_ref[...].astype(BF16))
    a = (_silu(g) * u).astype(BF16)
    for c in range(0, d, COL_CHUNK):
        o_ref[:, c:c + COL_CHUNK] += _dot(a, wo_ref[:, c:c + COL_CHUNK].astype(BF16))

    @pl.when(j == pl.num_programs(1) - 1)
    def _():
        def body(r, carry):
            rows = pl.ds(pl.multiple_of(r * ROW_CHUNK, ROW_CHUNK), ROW_CHUNK)
            o_ref[rows, :] = x_ref[rows, :] + (FFN_RESIDUAL * gate_ref[...]) * o_ref[rows, :]
            return carry
        lax.fori_loop(0, tm // ROW_CHUNK, body, 0)


def _ffn(x, gain, scale, shift, gate, w_in, w_out, lead, tm=1024, tf=256):
    s, d = x.shape
    f = w_out.shape[-2]
    tm = min(tm, s)
    nf = f // tf
    vec = pl.BlockSpec((1, d), lambda i, j: (0, 0))
    return pl.pallas_call(
        _ffn_kernel,
        grid=(s // tm, nf),
        in_specs=[
            pl.BlockSpec((tm, d), lambda i, j: (i, 0)),
            vec, vec, vec, vec,
            _wspec(lead, (d, tf), lambda i, j: (0, j)),
            _wspec(lead, (d, tf), lambda i, j: (0, j + nf)),
            _wspec(lead, (tf, d), lambda i, j: (j, 0)),
        ],
        out_specs=pl.BlockSpec((tm, d), lambda i, j: (i, 0)),
        out_shape=jax.ShapeDtypeStruct((s, d), F32),
        scratch_shapes=[pltpu.VMEM((tm, d), BF16)],
        compiler_params=_cparams(("arbitrary", "arbitrary")),
        name="ffn",
    )(x, gain, scale, shift, gate, w_in, w_in, w_out)


def _head_rms_norm(y, gain):
    outs = []
    for c in range(0, y.shape[1], ATTN_HEAD_DIM):
        yc = y[:, c:c + ATTN_HEAD_DIM]
        outs.append(yc * lax.rsqrt(jnp.mean(yc * yc, axis=-1, keepdims=True) + NORM_EPS))
    return jnp.concatenate(outs, axis=1) * gain


def _norm_kernel(x_ref, gain_ref, scale_ref, shift_ref, h_ref):
    def body(r, carry):
        rows = pl.ds(pl.multiple_of(r * NORM_ROWS, NORM_ROWS), NORM_ROWS)
        h = _mod_norm(x_ref[rows, :], gain_ref[...], scale_ref[...], shift_ref[...])
        h_ref[rows, :] = h.astype(BF16)
        return carry
    lax.fori_loop(0, x_ref.shape[0] // NORM_ROWS, body, 0)


def _norm(x, gain, scale, shift, tm=512):
    s, d = x.shape
    tm = min(tm, s)
    vec = pl.BlockSpec((1, d), lambda i: (0, 0))
    return pl.pallas_call(
        _norm_kernel,
        grid=(s // tm,),
        in_specs=[pl.BlockSpec((tm, d), lambda i: (i, 0)), vec, vec, vec],
        out_specs=pl.BlockSpec((tm, d), lambda i: (i, 0)),
        out_shape=jax.ShapeDtypeStruct((s, d), BF16),
        compiler_params=_cparams(("arbitrary",)),
        name="mod_norm",
    )(x, gain, scale, shift)


def _proj_kernel(h_ref, *refs, shift, head_norm, transpose_out):
    wa_ref, o_ref, w_scr = refs[0], refs[-2], refs[-1]
    tn = wa_ref.shape[0]

    @pl.when(pl.program_id(1) == 0)
    def _():
        w_scr[:tn - shift, :] = wa_ref[shift:, :].astype(BF16)
        if shift:
            w_scr[tn - shift:, :] = refs[1][:shift, :].astype(BF16)

    if transpose_out:
        o_ref[...] = _dot_nt(w_scr[...], h_ref[...]).astype(o_ref.dtype)
        return
    y = _dot_nt(h_ref[...], w_scr[...])
    if head_norm:
        y = _head_rms_norm(y, refs[-3][...])
    o_ref[...] = y.astype(o_ref.dtype)


PROJ_SHIFT_ROWS = 128


def _proj(h, wt, lead, row0, n, out_dtype, head_gain=None, transpose_out=False, tm=1024, tn=1024):
    s, d = h.shape
    tm = min(tm, s)
    tn = min(tn, n)
    base = row0 // tn * tn
    shift = row0 - base
    assert n % tn == 0 and shift % 16 == 0 and shift <= PROJ_SHIFT_ROWS and tn % PROJ_SHIFT_ROWS == 0
    specs = [pl.BlockSpec((tm, d), lambda j, i: (i, 0)),
             _wspec(lead, (tn, d), lambda j, i: (base // tn + j, 0))]
    args = [h, wt]
    if shift:
        specs.append(_wspec(lead, (PROJ_SHIFT_ROWS, d),
                            lambda j, i: ((base + (j + 1) * tn) // PROJ_SHIFT_ROWS, 0)))
        args.append(wt)
    if head_gain is not None:
        specs.append(pl.BlockSpec((1, tn), lambda j, i: (0, j)))
        args.append(head_gain)
    assert not (transpose_out and head_gain is not None)
    if transpose_out:
        out_spec, out_shape = pl.BlockSpec((tn, tm), lambda j, i: (j, i)), (n, s)
    else:
        out_spec, out_shape = pl.BlockSpec((tm, tn), lambda j, i: (i, j)), (s, n)
    return pl.pallas_call(
        functools.partial(_proj_kernel, shift=shift, head_norm=head_gain is not None,
                          transpose_out=transpose_out),
        grid=(n // tn, s // tm),
        in_specs=specs,
        out_specs=out_spec,
        out_shape=jax.ShapeDtypeStruct(out_shape, out_dtype),
        scratch_shapes=[pltpu.VMEM((tn, d), BF16)],
        compiler_params=_cparams(("arbitrary", "arbitrary")),
        name="proj",
    )(*args)


def _t5_bucket(dist):
    n = jnp.maximum(dist, 0)
    max_exact = REL_BUCKETS // 2
    ratio = jnp.log(jnp.maximum(n, 1).astype(F32) / max_exact) / math.log(REL_MAX_DIST / max_exact)
    large = max_exact + (ratio * (REL_BUCKETS - max_exact)).astype(jnp.int32)
    large = jnp.minimum(large, REL_BUCKETS - 1)
    return jnp.where(n < max_exact, n, large)


ATTN_VIS_STEPS = (4, 8, 12, 16, 20, 24, 28, 32)
ATTN_QK_AHEAD = 2
ATTN_HEADS_PER_STEP = 2
ATTN_SUM_ROWS = 16
LOG2E = 1.4426950408889634
ATTN_Q_PRESCALE = ATTN_HEAD_DIM ** -0.5 * LOG2E


def _moba_kernel(rel_ref, bkt_own_ref, bkt_prev_ref, q_ref, k_ref, vt_ref, o_ref,
                 kmean_scr, bias_own_scr, bias_prev_scr, vt_scr, mask_scr):
    hg = pl.program_id(0)
    i = pl.program_id(1)
    blk, hd, G = MOBA_BLOCK, ATTN_HEAD_DIM, ATTN_HEADS_PER_STEP
    nb = k_ref.shape[0] // blk
    heads = [hg * G + g for g in range(G)]
    cols = [slice(g * hd, (g + 1) * hd) for g in range(G)]

    @pl.when(i == 0)
    def _():
        s_len = k_ref.shape[0]
        avg = jnp.where(lax.shift_right_logical(lax.broadcasted_iota(jnp.int32, (LANES, s_len), 1),
                                                blk.bit_length() - 1)
                        == lax.broadcasted_iota(jnp.int32, (LANES, s_len), 0), 1.0 / blk, 0.0).astype(BF16)
        ones_row = jnp.where(lax.broadcasted_iota(jnp.int32, (ATTN_SUM_ROWS, blk), 0) == 0, 1.0, 0.0)
        key = lax.broadcasted_iota(jnp.int32, (blk, blk), 0)
        qry = lax.broadcasted_iota(jnp.int32, (blk, blk), 1)
        for g in range(G):
            kmean_scr[g] = _dot(avg, k_ref[:, cols[g]])
            for b in range(nb):
                vt_scr[g, b, :hd, :] = vt_ref[g * hd:(g + 1) * hd, b * blk:(b + 1) * blk]
                vt_scr[g, b, hd:, :] = ones_row.astype(BF16)

            bias_own_scr[g] = jnp.where(key <= qry, 0.0, NEG_INF)
            bias_prev_scr[g] = jnp.zeros((blk, blk), F32)

            def bias_body(b, carry, g=g):
                val = rel_ref[b, heads[g]] * LOG2E
                bias_own_scr[g] += jnp.where(bkt_own_ref[...] == b, val, 0.0)
                bias_prev_scr[g] += jnp.where(bkt_prev_ref[...] == b, val, 0.0)
                return carry
            lax.fori_loop(0, REL_BUCKETS, bias_body, 0)

    qs = [q_ref[:, cols[g]] for g in range(G)]
    for g in range(G):
        mask_scr[g] = _choose_blocks(i, nb, qs[g], kmean_scr[g])

    far_bias = [rel_ref[REL_BUCKETS - 1, heads[g]] * LOG2E for g in range(G)]
    visible = i + 1
    lo = 0
    for n_vis in sorted({min(v, nb) for v in ATTN_VIS_STEPS} | {nb}):
        @pl.when((visible > lo) & (visible <= n_vis))
        def _(n_vis=n_vis):
            _moba_tile(n_vis, nb, i, far_bias, qs, cols, k_ref, o_ref,
                       kmean_scr, bias_own_scr, bias_prev_scr, vt_scr, mask_scr)
        lo = n_vis


def _choose_blocks(i, nb, q, kmean):
    km_hi = kmean.astype(BF16)
    km_lo = (kmean - km_hi.astype(F32)).astype(BF16)
    gate = _dot_nt(km_hi, q) + _dot_nt(km_lo, q)
    rowb = lax.broadcasted_iota(jnp.int32, gate.shape, 0)
    rowf = rowb.astype(F32)
    gate = jnp.where(rowb < i, gate, -jnp.inf)
    sel = jnp.zeros(gate.shape, F32)
    for _ in range(min(MOBA_TOPK, nb)):
        top = jnp.max(gate, axis=0, keepdims=True)
        idx = jnp.min(jnp.where(gate == top, rowf, float(LANES)), axis=0, keepdims=True)
        hit = rowf == idx
        sel = jnp.where(hit & (top > -jnp.inf), 1.0, sel)
        gate = jnp.where(hit, -jnp.inf, gate)
    return jnp.where(sel > 0.0, 0.0, NEG_INF)


def _moba_tile(n_vis, nb, i, far_bias, qs, cols, k_ref, o_ref,
               kmean_scr, bias_own_scr, bias_prev_scr, vt_scr, mask_scr):
    blk, hd = MOBA_BLOCK, ATTN_HEAD_DIM
    heads = range(len(qs))
    qt = [q.astype(F32).T.astype(BF16) for q in qs]

    def rows(j):
        return pl.ds(pl.multiple_of(j * blk, blk), blk)

    n_far = n_vis - 2
    jp = jnp.maximum(i - 1, 0)
    s_own = [_dot(k_ref[rows(i), cols[g]], qt[g]) for g in heads]
    s_prev = [_dot(k_ref[rows(jp), cols[g]], qt[g]) for g in heads]

    def far_qk(j):
        return [_dot(k_ref[j * blk:(j + 1) * blk, cols[g]], qt[g]) for g in heads]
    far_s = {j: far_qk(j) for j in range(min(ATTN_QK_AHEAD, n_far))}

    m, acc = [], []
    for g in heads:
        chosen = jnp.where(i >= 1, mask_scr[g, pl.ds(jp, 1), :], NEG_INF)
        t_own = s_own[g] + bias_own_scr[g]
        t_prev = s_prev[g] + bias_prev_scr[g] + chosen
        m.append(jnp.maximum(jnp.max(t_own, axis=0, keepdims=True), jnp.max(t_prev, axis=0, keepdims=True)))
        p = jnp.concatenate([jnp.exp2(t_prev - m[g]), jnp.exp2(t_own - m[g])], axis=0).astype(BF16)
        acc.append(_dot(jnp.concatenate([vt_scr[g, jp], vt_scr[g, i]], axis=1), p))

    for j in range(n_far):
        s = far_s.pop(j)
        if j + ATTN_QK_AHEAD < n_far:
            far_s[j + ATTN_QK_AHEAD] = far_qk(j + ATTN_QK_AHEAD)
        for g in heads:
            shift = far_bias[g] + jnp.where(j < i - 1, mask_scr[g, j:j + 1, :], NEG_INF)
            m_new = jnp.maximum(m[g], jnp.max(s[g], axis=0, keepdims=True) + shift)
            p = jnp.exp2(s[g] + (shift - m_new))
            acc[g] = jnp.exp2(m[g] - m_new) * acc[g] + _dot(vt_scr[g, j], p.astype(BF16))
            m[g] = m_new

    for g in heads:
        o_ref[:, cols[g]] = (acc[g][:hd, :] / acc[g][hd:hd + 1, :]).T.astype(o_ref.dtype)


def _moba_attention(qk, vt, rel_bias):
    s = qk.shape[0]
    blk, hd, nh = MOBA_BLOCK, ATTN_HEAD_DIM, ATTN_HEADS
    assert s % blk == 0 and s // blk < LANES
    assert blk & (blk - 1) == 0
    assert blk >= REL_MAX_DIST
    r = jnp.arange(blk)
    dist_own = r[None, :] - r[:, None]
    bkt_own = _t5_bucket(dist_own)
    bkt_prev = _t5_bucket(dist_own + blk)
    const = lambda h, i: (0, 0)
    G = ATTN_HEADS_PER_STEP
    assert nh % G == 0
    gw = G * hd
    return pl.pallas_call(
        _moba_kernel,
        grid=(nh // G, s // blk),
        in_specs=[
            pl.BlockSpec(memory_space=pltpu.SMEM),
            pl.BlockSpec((blk, blk), const),
            pl.BlockSpec((blk, blk), const),
            pl.BlockSpec((blk, gw), lambda h, i: (i, h)),
            pl.BlockSpec((s, gw), lambda h, i: (0, nh // G + h)),
            pl.BlockSpec((gw, s), lambda h, i: (h, 0)),
        ],
        out_specs=pl.BlockSpec((blk, gw), lambda h, i: (i, h)),
        out_shape=jax.ShapeDtypeStruct((s, nh * hd), BF16),
        scratch_shapes=[
            pltpu.VMEM((G, LANES, hd), F32),
            pltpu.VMEM((G, blk, blk), F32),
            pltpu.VMEM((G, blk, blk), F32),
            pltpu.VMEM((G, s // blk, hd + ATTN_SUM_ROWS, blk), BF16),
            pltpu.VMEM((G, LANES, blk), F32),
        ],
        compiler_params=_cparams(("arbitrary", "arbitrary")),
        name="moba_attention",
    )(rel_bias, bkt_own, bkt_prev, qk, qk, vt)


SSD_TAIL = 8
SSD_CONV_COLS = 512


def _softplus(x):
    return jnp.maximum(x, 0.0) + jnp.log(1.0 + jnp.exp(-jnp.abs(x)))


def _split3(x):
    hi = x.astype(BF16)
    r1 = x - hi.astype(F32)
    mid = r1.astype(BF16)
    lo = (r1 - mid.astype(F32)).astype(BF16)
    return hi, mid, lo


def _ssd_kernel(xbc_ref, z_ref, dt_ref, dtt_ref, wconv_ref, bconv_ref, dtb_ref, alog_ref,
                dtbt_ref, alogt_ref, dskip_ref, norm_ref, y_ref,
                tail_scr, xact_scr, state_scr):
    c = pl.program_id(0)
    L, P, N, W = SSD_CHUNK, SSD_HEAD_DIM, SSD_STATE, SSD_WIDTH
    assert P & (P - 1) == 0

    @pl.when(c == 0)
    def _():
        tail_scr[...] = jnp.zeros(tail_scr.shape, F32)
        state_scr[...] = jnp.zeros(state_scr.shape, F32)

    rr = lax.broadcasted_iota(jnp.int32, (L, L), 0)
    cc = lax.broadcasted_iota(jnp.int32, (L, L), 1)
    causal = rr >= cc

    shifts = [jnp.where(rr - cc == k, 1.0, 0.0).astype(BF16) for k in range(1, SSD_CONV)]
    row8 = lax.broadcasted_iota(jnp.int32, (SSD_TAIL, SSD_CONV_COLS), 0)
    for c0 in range(0, SSD_XBC, SSD_CONV_COLS):
        cols = slice(c0, c0 + SSD_CONV_COLS)
        x = xbc_ref[:, cols]
        tail = tail_scr[:, cols]
        acc = bconv_ref[:, cols] + wconv_ref[SSD_CONV - 1:SSD_CONV, cols] * x.astype(F32)
        fix = jnp.zeros((SSD_TAIL, SSD_CONV_COLS), F32)
        for k in range(1, SSD_CONV):
            wk = wconv_ref[SSD_CONV - 1 - k:SSD_CONV - k, cols]
            acc = acc + wk * _dot(shifts[k - 1], x)
            fix = fix + wk * jnp.where(row8 < k, pltpu.roll(tail, k, 0), 0.0)
        acc = jnp.concatenate([acc[:SSD_TAIL] + fix, acc[SSD_TAIL:]], axis=0)
        xact_scr[:, cols] = _silu(acc)
        tail_scr[:, cols] = x[L - SSD_TAIL:, :].astype(F32)

    dtv = _softplus(dt_ref[...] + dtb_ref[...])
    ad = dtv * (-jnp.exp(alog_ref[...]))
    dtt = _softplus(dtt_ref[...] + dtbt_ref[...])
    adt = dtt * (-jnp.exp(alogt_ref[...]))
    lower = jnp.where(causal, 1.0, 0.0).astype(BF16)
    upper = jnp.where(rr <= cc, 1.0, 0.0).astype(BF16)
    a_cs = sum(_dot(lower, t) for t in _split3(ad))
    a_cst = sum(_dot(t, upper) for t in _split3(adt))
    last = a_cs[L - 1:L, :]

    factors = jnp.concatenate([dtv, jnp.exp(a_cs), jnp.exp(last - a_cs)], axis=0).astype(BF16)
    group_w = W // SSD_GROUPS
    heads_per_group = SSD_HEADS // SSD_GROUPS
    e_row = lax.broadcasted_iota(jnp.int32, (LANES, group_w), 0)
    e_head = lax.shift_right_logical(lax.broadcasted_iota(jnp.int32, (LANES, group_w), 1),
                                     P.bit_length() - 1)
    pair_lane = lax.broadcasted_iota(jnp.int32, (L, 2 * P), 1)

    for g in range(SSD_GROUPS):
        gc = slice(g * group_w, (g + 1) * group_w)
        expand = jnp.where(e_row == g * heads_per_group + e_head, 1.0, 0.0).astype(BF16)
        spread = _dot(factors, expand)
        xs = xact_scr[:, gc]
        xd = xs * spread[:L]
        xd_b = xd.astype(BF16)
        xe_b = (xd * spread[2 * L:]).astype(BF16)
        bg = xact_scr[:, W + g * N:W + (g + 1) * N].astype(BF16)
        cg = xact_scr[:, W + SSD_BC + g * N:W + SSD_BC + (g + 1) * N].astype(BF16)
        st = state_scr[gc, :]
        y_off = _dot_nt(cg, st.astype(BF16)) * spread[L:2 * L]
        new = lax.dot_general(xe_b, bg, (((0,), (0,)), ((), ())), preferred_element_type=F32)
        cb = _dot_nt(cg, bg)
        pairs = []
        for r in range(0, heads_per_group, 2):
            h = g * heads_per_group + r
            xd_pair = xd_b[:, r * P:(r + 2) * P]
            halves = []
            for hh in (h, h + 1):
                dec = jnp.exp(jnp.where(causal, a_cs[:, hh:hh + 1] - a_cst[hh:hh + 1, :], -jnp.inf))
                halves.append(_dot((cb * dec).astype(BF16), xd_pair))
                hr = slice((hh - g * heads_per_group) * P, (hh - g * heads_per_group + 1) * P)
                state_scr[g * group_w + hr.start:g * group_w + hr.stop, :] = (
                    jnp.exp(a_cst[hh:hh + 1, L - 1:L]) * st[hr, :] + new[hr, :])
            pairs.append(jnp.where(pair_lane < P, halves[0], halves[1]))
        y = jnp.concatenate(pairs, axis=1) + y_off + xs * dskip_ref[:, gc]
        y = y * _silu(z_ref[:, gc].astype(F32))
        y = y * lax.rsqrt(jnp.mean(y * y, axis=-1, keepdims=True) + NORM_EPS)
        y_ref[:, gc] = (y * norm_ref[:, gc]).astype(y_ref.dtype)


def _ssd(xbc, z, dt, w_conv, b_conv, dt_bias, a_log, d_skip, norm):
    s = xbc.shape[0]
    L, nh = SSD_CHUNK, SSD_HEADS
    assert s % L == 0
    dtt = dt[:, :nh].T
    pad = lambda v: jnp.pad(v, (0, LANES - nh)).reshape(1, LANES)
    full = lambda r, cdim: pl.BlockSpec((r, cdim), lambda c: (0, 0))
    return pl.pallas_call(
        _ssd_kernel,
        grid=(s // L,),
        in_specs=[
            pl.BlockSpec((L, SSD_XBC), lambda c: (c, 0)),
            pl.BlockSpec((L, SSD_WIDTH), lambda c: (c, 0)),
            pl.BlockSpec((L, LANES), lambda c: (c, 0)),
            pl.BlockSpec((nh, L), lambda c: (0, c)),
            full(SSD_CONV, SSD_XBC), full(1, SSD_XBC),
            full(1, LANES), full(1, LANES), full(nh, 1), full(nh, 1),
            full(1, SSD_WIDTH), full(1, SSD_WIDTH),
        ],
        out_specs=pl.BlockSpec((L, SSD_WIDTH), lambda c: (c, 0)),
        out_shape=jax.ShapeDtypeStruct((s, SSD_WIDTH), BF16),
        scratch_shapes=[
            pltpu.VMEM((SSD_TAIL, SSD_XBC), F32),
            pltpu.VMEM((L, SSD_XBC), F32),
            pltpu.VMEM((SSD_WIDTH, SSD_STATE), F32),
        ],
        compiler_params=_cparams(("arbitrary",)),
        name="ssd_scan",
    )(xbc, z, dt, dtt, w_conv, b_conv.reshape(1, SSD_XBC), pad(dt_bias), pad(a_log),
      dt_bias.reshape(nh, 1), a_log.reshape(nh, 1),
      jnp.repeat(d_skip, SSD_HEAD_DIM).reshape(1, SSD_WIDTH), norm.reshape(1, SSD_WIDTH))


HALO = 16


def _merge_kernel(ya_ref, ys_ref, cb_ref, cc_ref, cx_ref, hc_ref, hx_ref, ga_ref, gs_ref, gc_ref,
                  wsc_ref, wa_ref, ws_ref, wc_ref, o_ref, ext_scr, yc_scr):
    i = pl.program_id(0)
    j = pl.program_id(1)
    tm = ya_ref.shape[0]

    @pl.when(j == 0)
    def _():
        halo = hc_ref[...].astype(F32) * hx_ref[...].astype(F32)
        ext_scr[0:HALO, :] = jnp.where(i > 0, halo, 0.0)
        for r0 in range(0, tm, ROW_CHUNK):
            rows = slice(r0, r0 + ROW_CHUNK)
            ext_scr[HALO + r0:HALO + r0 + ROW_CHUNK, :] = cc_ref[rows, :].astype(F32) * cx_ref[rows, :].astype(F32)
        for r0 in range(0, tm, ROW_CHUNK):
            rows = slice(r0, r0 + ROW_CHUNK)
            acc = jnp.zeros((ROW_CHUNK, CONV_WIDTH), F32)
            for k in range(CONV_K):
                start = HALO + r0 - (CONV_K - 1) + k
                acc = acc + wsc_ref[k:k + 1, :] * ext_scr[start:start + ROW_CHUNK, :]
            yc_scr[rows, :] = (cb_ref[rows, :].astype(F32) * acc).astype(BF16)

    merged = (_sigmoid(ga_ref[...].astype(F32)) * _dot(ya_ref[...], wa_ref[...].astype(BF16))
              + _sigmoid(gs_ref[...].astype(F32)) * _dot(ys_ref[...], ws_ref[...].astype(BF16))
              + _sigmoid(gc_ref[...].astype(F32)) * _dot(yc_scr[...], wc_ref[...].astype(BF16)))
    o_ref[...] = merged.astype(o_ref.dtype)


def _branch_merge(y_attn, y_ssd, tail, w_sc, w_a, w_s, w_c, lead, tm=1024, tn=512):
    s = y_attn.shape[0]
    d = w_a.shape[-1]
    tm = min(tm, s)
    cw = CONV_WIDTH
    g0 = 3 * cw // tn
    gd = d // tn
    row = lambda width, cb: pl.BlockSpec((tm, width), lambda i, j: (i, cb))
    halo = lambda cb: pl.BlockSpec((HALO, cw), lambda i, j: (jnp.maximum(i * (tm // HALO) - 1, 0), cb))
    gate = lambda k: pl.BlockSpec((tm, tn), lambda i, j: (i, g0 + k * gd + j))
    wcol = lambda kdim: _wspec(lead, (kdim, tn), lambda i, j: (0, j))
    return pl.pallas_call(
        _merge_kernel,
        grid=(s // tm, d // tn),
        in_specs=[
            row(ATTN_WIDTH, 0), row(SSD_WIDTH, 0),
            row(cw, 0), row(cw, 1), row(cw, 2), halo(1), halo(2),
            gate(0), gate(1), gate(2),
            pl.BlockSpec((CONV_K, cw), lambda i, j: (0, 0)),
            wcol(ATTN_WIDTH), wcol(SSD_WIDTH), wcol(cw),
        ],
        out_specs=pl.BlockSpec((tm, tn), lambda i, j: (i, j)),
        out_shape=jax.ShapeDtypeStruct((s, d), BF16),
        scratch_shapes=[pltpu.VMEM((HALO + tm, cw), F32), pltpu.VMEM((tm, cw), BF16)],
        compiler_params=_cparams(("arbitrary", "arbitrary")),
        name="branch_merge",
    )(y_attn, y_ssd, tail, tail, tail, tail, tail, tail, tail, tail, w_sc, w_a, w_s, w_c)


def _outproj_kernel(m_ref, w_ref, x_ref, gate_ref, o_ref):
    o_ref[...] = x_ref[...] + gate_ref[...] * _dot(m_ref[...], w_ref[...].astype(BF16))


def _out_proj(merged, w, lead, x, gate, tm=1024, tn=1024):
    s, d = x.shape
    tm = min(tm, s)
    return pl.pallas_call(
        _outproj_kernel,
        grid=(d // tn, s // tm),
        in_specs=[
            pl.BlockSpec((tm, merged.shape[1]), lambda j, i: (i, 0)),
            _wspec(lead, (merged.shape[1], tn), lambda j, i: (0, j)),
            pl.BlockSpec((tm, tn), lambda j, i: (i, j)),
            pl.BlockSpec((1, tn), lambda j, i: (0, j)),
        ],
        out_specs=pl.BlockSpec((tm, tn), lambda j, i: (i, j)),
        out_shape=jax.ShapeDtypeStruct((s, d), F32),
        compiler_params=_cparams(("arbitrary", "arbitrary")),
        name="out_proj",
    )(merged, w, x, gate)


def _token_mix(x, gain, scale, shift, gate, l, w_mix_in, qk_norm, rel_bias, w_ssd_conv, b_ssd_conv,
               ssd_dt_bias, ssd_a_log, ssd_d, ssd_norm, w_sc_conv, w_br_attn, w_br_ssd, w_br_conv,
               w_mix_out):
    d = x.shape[1]
    lead = (l,)
    c_v = 2 * ATTN_WIDTH
    c_z = c_v + ATTN_WIDTH
    c_xbc = c_z + SSD_WIDTH
    c_dt = c_xbc + SSD_XBC
    c_tail = c_dt + SSD_HEADS
    n_tail = 3 * CONV_WIDTH + 3 * d
    assert w_mix_in.shape[-1] == c_tail + n_tail

    hgain = jnp.concatenate([jnp.tile(qk_norm[l, 0] * ATTN_Q_PRESCALE, ATTN_HEADS),
                             jnp.tile(qk_norm[l, 1], ATTN_HEADS)])[None]
    wt = jnp.swapaxes(w_mix_in, 1, 2)
    h = _norm(x, gain, scale, shift)
    qk = _proj(h, wt, lead, 0, 2 * ATTN_WIDTH, BF16, head_gain=hgain)
    vt = _proj(h, wt, lead, c_v, ATTN_WIDTH, BF16, transpose_out=True)
    z = _proj(h, wt, lead, c_z, SSD_WIDTH, BF16)
    xbc = _proj(h, wt, lead, c_xbc, SSD_XBC, BF16)
    wt_dt = jnp.pad(wt[l, c_dt:c_tail, :], ((0, LANES - SSD_HEADS), (0, 0)))
    dt = _proj(h, wt_dt, (), 0, LANES, F32)
    tail = _proj(h, wt, lead, c_tail, n_tail, BF16)

    y_attn = _moba_attention(qk, vt, rel_bias)
    y_ssd = _ssd(xbc, z, dt, w_ssd_conv[l], b_ssd_conv[l], ssd_dt_bias[l], ssd_a_log[l], ssd_d[l], ssd_norm[l])
    merged = _branch_merge(y_attn, y_ssd, tail, w_sc_conv[l], w_br_attn[l].astype(BF16),
                           w_br_ssd[l].astype(BF16), w_br_conv[l].astype(BF16), ())
    return _out_proj(merged, w_mix_out, lead, x, gate)


def kernel(x, c, w_ada, b_ada, norm_gain, w_ffn_in, w_ffn_out, w_mix_in, qk_norm, rel_bias, w_ssd_conv, b_ssd_conv, ssd_dt_bias, ssd_a_log, ssd_d, ssd_norm, w_sc_conv, w_br_attn, w_br_ssd, w_br_conv, w_mix_out):
    b, s, d = x.shape
    depth = w_ada.shape[0]
    assert b == 1 and s % math.lcm(MOBA_BLOCK, SSD_CHUNK) == 0
    xs = x.reshape(s, d)
    ada = _ada_proj(c, w_ada, b_ada).reshape(depth, N_SUBLAYERS, 3, 1, d)
    for l in range(depth):
        mod = lambda i: (norm_gain[l, i][None], ada[l, i, 1], ada[l, i, 0], ada[l, i, 2])
        xs = _ffn(xs, *mod(0), w_ffn_in, w_ffn_out, (l, 0))
        xs = _token_mix(xs, *mod(1), l, w_mix_in, qk_norm, rel_bias, w_ssd_conv, b_ssd_conv,
                        ssd_dt_bias, ssd_a_log, ssd_d, ssd_norm, w_sc_conv,
                        w_br_attn, w_br_ssd, w_br_conv, w_mix_out)
        xs = _ffn(xs, *mod(2), w_ffn_in, w_ffn_out, (l, 1))
    return xs.reshape(b, s, d)
```

```python
import functools
import math

import jax
import jax.numpy as jnp
from jax import lax
from jax.experimental import pallas as pl
from jax.experimental.pallas import tpu as pltpu

F32 = jnp.float32
BF16 = jnp.bfloat16

ATTN_HEADS = 8
ATTN_HEAD_DIM = 128
ATTN_WIDTH = ATTN_HEADS * ATTN_HEAD_DIM
MOBA_BLOCK = 256
MOBA_TOPK = 3
REL_BUCKETS = 32
REL_MAX_DIST = 128
SSD_HEADS = 32
SSD_HEAD_DIM = 64
SSD_WIDTH = SSD_HEADS * SSD_HEAD_DIM
SSD_GROUPS = 4
SSD_STATE = 128
SSD_CONV = 4
SSD_CHUNK = 256
SSD_BC = SSD_GROUPS * SSD_STATE
SSD_XBC = SSD_WIDTH + 2 * SSD_BC
CONV_WIDTH = 1024
CONV_K = 3
N_SUBLAYERS = 3
FFN_RESIDUAL = 0.5
NORM_EPS = 1e-6
NEG_INF = -1e30

LANES = 128
VMEM_LIMIT_BYTES = 58 * 1024 * 1024


def _cparams(semantics):
    return pltpu.CompilerParams(dimension_semantics=semantics, vmem_limit_bytes=VMEM_LIMIT_BYTES)


def _sigmoid(x):
    return 1.0 / (1.0 + jnp.exp(-x))


def _silu(x):
    return x * _sigmoid(x)


def _dot(a, b):
    return jnp.dot(a, b, preferred_element_type=F32)


def _dot_nt(a, b):
    return lax.dot_general(a, b, (((1,), (1,)), ((), ())), preferred_element_type=F32)


def _wspec(lead, block, index_map):
    return pl.BlockSpec((None,) * len(lead) + block, lambda i, j: lead + index_map(i, j))


def _mod_norm(x, gain, scale, shift):
    y = x * lax.rsqrt(jnp.mean(x * x, axis=-1, keepdims=True) + NORM_EPS)
    return (y * gain) * (1.0 + scale) + shift


def _ada_kernel(c_ref, w_ref, b_ref, o_ref):
    cond = _silu(c_ref[...]).astype(BF16)
    o_ref[...] = _dot(cond, w_ref[...].astype(BF16)) + b_ref[...]


def _ada_proj(c, w_ada, b_ada, tn=1024):
    depth, d, n = w_ada.shape
    c8 = jnp.broadcast_to(c, (8, d))
    out = pl.pallas_call(
        _ada_kernel,
        grid=(depth, n // tn),
        in_specs=[
            pl.BlockSpec((8, d), lambda l, j: (0, 0)),
            pl.BlockSpec((None, d, tn), lambda l, j: (l, 0, j)),
            pl.BlockSpec((None, 1, tn), lambda l, j: (l, 0, j)),
        ],
        out_specs=pl.BlockSpec((None, 8, tn), lambda l, j: (l, 0, j)),
        out_shape=jax.ShapeDtypeStruct((depth, 8, n), F32),
        compiler_params=_cparams(("arbitrary", "arbitrary")),
        name="ada_proj",
    )(c8, w_ada, b_ada.reshape(depth, 1, n))
    return out[:, 0, :]


ROW_CHUNK = 128
NORM_ROWS = 128
COL_CHUNK = 512


def _ffn_kernel(x_ref, gain_ref, scale_ref, shift_ref, gate_ref, wg_ref, wu_ref, wo_ref,
                o_ref, h_scr):
    j = pl.program_id(1)
    tm, d = x_ref.shape

    @pl.when(j == 0)
    def _():
        def body(r, carry):
            rows = pl.ds(pl.multiple_of(r * NORM_ROWS, NORM_ROWS), NORM_ROWS)
            h = _mod_norm(x_ref[rows, :], gain_ref[...], scale_ref[...], shift_ref[...])
            h_scr[rows, :] = h.astype(BF16)
            o_ref[rows, :] = jnp.zeros((NORM_ROWS, d), F32)
            return carry
        lax.fori_loop(0, tm // NORM_ROWS, body, 0)

    h = h_scr[...]
    g = _dot(h, wg_ref[...].astype(BF16))
    u = _dot(h, wu_ref[...].astype(BF16))
    a = (_silu(g) * u).astype(BF16)
    for c in range(0, d, COL_CHUNK):
        o_ref[:, c:c + COL_CHUNK] += _dot(a, wo_ref[:, c:c + COL_CHUNK].astype(BF16))

    @pl.when(j == pl.num_programs(1) - 1)
    def _():
        def body(r, carry):
            rows = pl.ds(pl.multiple_of(r * ROW_CHUNK, ROW_CHUNK), ROW_CHUNK)
            o_ref[rows, :] = x_ref[rows, :] + (FFN_RESIDUAL * gate_ref[...]) * o_ref[rows, :]
            return carry
        lax.fori_loop(0, tm // ROW_CHUNK, body, 0)


def _ffn(x, gain, scale, shift, gate, w_in, w_out, lead, tm=1024, tf=256):
    s, d = x.shape
    f = w_out.shape[-2]
    tm = min(tm, s)
    nf = f // tf
    vec = pl.BlockSpec((1, d), lambda i, j: (0, 0))
    return pl.pallas_call(
        _ffn_kernel,
        grid=(s // tm, nf),
        in_specs=[
            pl.BlockSpec((tm, d), lambda i, j: (i, 0)),
            vec, vec, vec, vec,
            _wspec(lead, (d, tf), lambda i, j: (0, j)),
            _wspec(lead, (d, tf), lambda i, j: (0, j + nf)),
            _wspec(lead, (tf, d), lambda i, j: (j, 0)),
        ],
        out_specs=pl.BlockSpec((tm, d), lambda i, j: (i, 0)),
        out_shape=jax.ShapeDtypeStruct((s, d), F32),
        scratch_shapes=[pltpu.VMEM((tm, d), BF16)],
        compiler_params=_cparams(("arbitrary", "arbitrary")),
        name="ffn",
    )(x, gain, scale, shift, gate, w_in, w_in, w_out)


def _head_rms_norm(y, gain):
    outs = []
    for c in range(0, y.shape[1], ATTN_HEAD_DIM):
        yc = y[:, c:c + ATTN_HEAD_DIM]
        outs.append(yc * lax.rsqrt(jnp.mean(yc * yc, axis=-1, keepdims=True) + NORM_EPS))
    return jnp.concatenate(outs, axis=1) * gain


def _norm_kernel(x_ref, gain_ref, scale_ref, shift_ref, h_ref):
    def body(r, carry):
        rows = pl.ds(pl.multiple_of(r * NORM_ROWS, NORM_ROWS), NORM_ROWS)
        h = _mod_norm(x_ref[rows, :], gain_ref[...], scale_ref[...], shift_ref[...])
        h_ref[rows, :] = h.astype(BF16)
        return carry
    lax.fori_loop(0, x_ref.shape[0] // NORM_ROWS, body, 0)


def _norm(x, gain, scale, shift, tm=512):
    s, d = x.shape
    tm = min(tm, s)
    vec = pl.BlockSpec((1, d), lambda i: (0, 0))
    return pl.pallas_call(
        _norm_kernel,
        grid=(s // tm,),
        in_specs=[pl.BlockSpec((tm, d), lambda i: (i, 0)), vec, vec, vec],
        out_specs=pl.BlockSpec((tm, d), lambda i: (i, 0)),
        out_shape=jax.ShapeDtypeStruct((s, d), BF16),
        compiler_params=_cparams(("arbitrary",)),
        name="mod_norm",
    )(x, gain, scale, shift)


def _proj_kernel(h_ref, *refs, shift, head_norm, transpose_out):
    wa_ref, o_ref, w_scr = refs[0], refs[-2], refs[-1]
    tn = wa_ref.shape[0]

    @pl.when(pl.program_id(1) == 0)
    def _():
        w_scr[:tn - shift, :] = wa_ref[shift:, :].astype(BF16)
        if shift:
            w_scr[tn - shift:, :] = refs[1][:shift, :].astype(BF16)

    if transpose_out:
        o_ref[...] = _dot_nt(w_scr[...], h_ref[...]).astype(o_ref.dtype)
        return
    y = _dot_nt(h_ref[...], w_scr[...])
    if head_norm:
        y = _head_rms_norm(y, refs[-3][...])
    o_ref[...] = y.astype(o_ref.dtype)


PROJ_SHIFT_ROWS = 128


def _proj(h, wt, lead, row0, n, out_dtype, head_gain=None, transpose_out=False, tm=1024, tn=1024):
    s, d = h.shape
    tm = min(tm, s)
    tn = min(tn, n)
    base = row0 // tn * tn
    shift = row0 - base
    assert n % tn == 0 and shift % 16 == 0 and shift <= PROJ_SHIFT_ROWS and tn % PROJ_SHIFT_ROWS == 0
    specs = [pl.BlockSpec((tm, d), lambda j, i: (i, 0)),
             _wspec(lead, (tn, d), lambda j, i: (base // tn + j, 0))]
    args = [h, wt]
    if shift:
        specs.append(_wspec(lead, (PROJ_SHIFT_ROWS, d),
                            lambda j, i: ((base + (j + 1) * tn) // PROJ_SHIFT_ROWS, 0)))
        args.append(wt)
    if head_gain is not None:
        specs.append(pl.BlockSpec((1, tn), lambda j, i: (0, j)))
        args.append(head_gain)
    assert not (transpose_out and head_gain is not None)
    if transpose_out:
        out_spec, out_shape = pl.BlockSpec((tn, tm), lambda j, i: (j, i)), (n, s)
    else:
        out_spec, out_shape = pl.BlockSpec((tm, tn), lambda j, i: (i, j)), (s, n)
    return pl.pallas_call(
        functools.partial(_proj_kernel, shift=shift, head_norm=head_gain is not None,
                          transpose_out=transpose_out),
        grid=(n // tn, s // tm),
        in_specs=specs,
        out_specs=out_spec,
        out_shape=jax.ShapeDtypeStruct(out_shape, out_dtype),
        scratch_shapes=[pltpu.VMEM((tn, d), BF16)],
        compiler_params=_cparams(("arbitrary", "arbitrary")),
        name="proj",
    )(*args)


def _t5_bucket(dist):
    n = jnp.maximum(dist, 0)
    max_exact = REL_BUCKETS // 2
    ratio = jnp.log(jnp.maximum(n, 1).astype(F32) / max_exact) / math.log(REL_MAX_DIST / max_exact)
    large = max_exact + (ratio * (REL_BUCKETS - max_exact)).astype(jnp.int32)
    large = jnp.minimum(large, REL_BUCKETS - 1)
    return jnp.where(n < max_exact, n, large)


ATTN_VIS_STEPS = (4, 8, 12, 16, 20, 24, 28, 32)
ATTN_QK_AHEAD = 2
ATTN_HEADS_PER_STEP = 2
ATTN_SUM_ROWS = 16
LOG2E = 1.4426950408889634
ATTN_Q_PRESCALE = ATTN_HEAD_DIM ** -0.5 * LOG2E


def _moba_kernel(rel_ref, bkt_own_ref, bkt_prev_ref, q_ref, k_ref, vt_ref, o_ref,
                 kmean_scr, bias_own_scr, bias_prev_scr, vt_scr, mask_scr):
    hg = pl.program_id(0)
    i = pl.program_id(1)
    blk, hd, G = MOBA_BLOCK, ATTN_HEAD_DIM, ATTN_HEADS_PER_STEP
    nb = k_ref.shape[0] // blk
    heads = [hg * G + g for g in range(G)]
    cols = [slice(g * hd, (g + 1) * hd) for g in range(G)]

    @pl.when(i == 0)
    def _():
        s_len = k_ref.shape[0]
        avg = jnp.where(lax.shift_right_logical(lax.broadcasted_iota(jnp.int32, (LANES, s_len), 1),
                                                blk.bit_length() - 1)
                        == lax.broadcasted_iota(jnp.int32, (LANES, s_len), 0), 1.0 / blk, 0.0).astype(BF16)
        ones_row = jnp.where(lax.broadcasted_iota(jnp.int32, (ATTN_SUM_ROWS, blk), 0) == 0, 1.0, 0.0)
        key = lax.broadcasted_iota(jnp.int32, (blk, blk), 0)
        qry = lax.broadcasted_iota(jnp.int32, (blk, blk), 1)
        for g in range(G):
            kmean_scr[g] = _dot(avg, k_ref[:, cols[g]])
            for b in range(nb):
                vt_scr[g, b, :hd, :] = vt_ref[g * hd:(g + 1) * hd, b * blk:(b + 1) * blk]
                vt_scr[g, b, hd:, :] = ones_row.astype(BF16)

            bias_own_scr[g] = jnp.where(key <= qry, 0.0, NEG_INF)
            bias_prev_scr[g] = jnp.zeros((blk, blk), F32)

            def bias_body(b, carry, g=g):
                val = rel_ref[b, heads[g]] * LOG2E
                bias_own_scr[g] += jnp.where(bkt_own_ref[...] == b, val, 0.0)
                bias_prev_scr[g] += jnp.where(bkt_prev_ref[...] == b, val, 0.0)
                return carry
            lax.fori_loop(0, REL_BUCKETS, bias_body, 0)

    qs = [q_ref[:, cols[g]] for g in range(G)]
    for g in range(G):
        mask_scr[g] = _choose_blocks(i, nb, qs[g], kmean_scr[g])

    far_bias = [rel_ref[REL_BUCKETS - 1, heads[g]] * LOG2E for g in range(G)]
    visible = i + 1
    lo = 0
    for n_vis in sorted({min(v, nb) for v in ATTN_VIS_STEPS} | {nb}):
        @pl.when((visible > lo) & (visible <= n_vis))
        def _(n_vis=n_vis):
            _moba_tile(n_vis, nb, i, far_bias, qs, cols, k_ref, o_ref,
                       kmean_scr, bias_own_scr, bias_prev_scr, vt_scr, mask_scr)
        lo = n_vis


def _choose_blocks(i, nb, q, kmean):
    km_hi = kmean.astype(BF16)
    km_lo = (kmean - km_hi.astype(F32)).astype(BF16)
    gate = _dot_nt(km_hi, q) + _dot_nt(km_lo, q)
    rowb = lax.broadcasted_iota(jnp.int32, gate.shape, 0)
    rowf = rowb.astype(F32)
    gate = jnp.where(rowb < i, gate, -jnp.inf)
    sel = jnp.zeros(gate.shape, F32)
    for _ in range(min(MOBA_TOPK, nb)):
        top = jnp.max(gate, axis=0, keepdims=True)
        idx = jnp.min(jnp.where(gate == top, rowf, float(LANES)), axis=0, keepdims=True)
        hit = rowf == idx
        sel = jnp.where(hit & (top > -jnp.inf), 1.0, sel)
        gate = jnp.where(hit, -jnp.inf, gate)
    return jnp.where(sel > 0.0, 0.0, NEG_INF)


def _moba_tile(n_vis, nb, i, far_bias, qs, cols, k_ref, o_ref,
               kmean_scr, bias_own_scr, bias_prev_scr, vt_scr, mask_scr):
    blk, hd = MOBA_BLOCK, ATTN_HEAD_DIM
    heads = range(len(qs))
    qt = [q.astype(F32).T.astype(BF16) for q in qs]

    def rows(j):
        return pl.ds(pl.multiple_of(j * blk, blk), blk)

    n_far = n_vis - 2
    jp = jnp.maximum(i - 1, 0)
    s_own = [_dot(k_ref[rows(i), cols[g]], qt[g]) for g in heads]
    s_prev = [_dot(k_ref[rows(jp), cols[g]], qt[g]) for g in heads]

    def far_qk(j):
        return [_dot(k_ref[j * blk:(j + 1) * blk, cols[g]], qt[g]) for g in heads]
    far_s = {j: far_qk(j) for j in range(min(ATTN_QK_AHEAD, n_far))}

    m, acc = [], []
    for g in heads:
        chosen = jnp.where(i >= 1, mask_scr[g, pl.ds(jp, 1), :], NEG_INF)
        t_own = s_own[g] + bias_own_scr[g]
        t_prev = s_prev[g] + bias_prev_scr[g] + chosen
        m.append(jnp.maximum(jnp.max(t_own, axis=0, keepdims=True), jnp.max(t_prev, axis=0, keepdims=True)))
        p = jnp.concatenate([jnp.exp2(t_prev - m[g]), jnp.exp2(t_own - m[g])], axis=0).astype(BF16)
        acc.append(_dot(jnp.concatenate([vt_scr[g, jp], vt_scr[g, i]], axis=1), p))

    for j in range(n_far):
        s = far_s.pop(j)
        if j + ATTN_QK_AHEAD < n_far:
            far_s[j + ATTN_QK_AHEAD] = far_qk(j + ATTN_QK_AHEAD)
        for g in heads:
            shift = far_bias[g] + jnp.where(j < i - 1, mask_scr[g, j:j + 1, :], NEG_INF)
            m_new = jnp.maximum(m[g], jnp.max(s[g], axis=0, keepdims=True) + shift)
            p = jnp.exp2(s[g] + (shift - m_new))
            acc[g] = jnp.exp2(m[g] - m_new) * acc[g] + _dot(vt_scr[g, j], p.astype(BF16))
            m[g] = m_new

    for g in heads:
        o_ref[:, cols[g]] = (acc[g][:hd, :] / acc[g][hd:hd + 1, :]).T.astype(o_ref.dtype)


def _moba_attention(qk, vt, rel_bias):
    s = qk.shape[0]
    blk, hd, nh = MOBA_BLOCK, ATTN_HEAD_DIM, ATTN_HEADS
    assert s % blk == 0 and s // blk < LANES
    assert blk & (blk - 1) == 0
    assert blk >= REL_MAX_DIST
    r = jnp.arange(blk)
    dist_own = r[None, :] - r[:, None]
    bkt_own = _t5_bucket(dist_own)
    bkt_prev = _t5_bucket(dist_own + blk)
    const = lambda h, i: (0, 0)
    G = ATTN_HEADS_PER_STEP
    assert nh % G == 0
    gw = G * hd
    return pl.pallas_call(
        _moba_kernel,
        grid=(nh // G, s // blk),
        in_specs=[
            pl.BlockSpec(memory_space=pltpu.SMEM),
            pl.BlockSpec((blk, blk), const),
            pl.BlockSpec((blk, blk), const),
            pl.BlockSpec((blk, gw), lambda h, i: (i, h)),
            pl.BlockSpec((s, gw), lambda h, i: (0, nh // G + h)),
            pl.BlockSpec((gw, s), lambda h, i: (h, 0)),
        ],
        out_specs=pl.BlockSpec((blk, gw), lambda h, i: (i, h)),
        out_shape=jax.ShapeDtypeStruct((s, nh * hd), BF16),
        scratch_shapes=[
            pltpu.VMEM((G, LANES, hd), F32),
            pltpu.VMEM((G, blk, blk), F32),
            pltpu.VMEM((G, blk, blk), F32),
            pltpu.VMEM((G, s // blk, hd + ATTN_SUM_ROWS, blk), BF16),
            pltpu.VMEM((G, LANES, blk), F32),
        ],
        compiler_params=_cparams(("arbitrary", "arbitrary")),
        name="moba_attention",
    )(rel_bias, bkt_own, bkt_prev, qk, qk, vt)


SSD_TAIL = 8
SSD_CONV_COLS = 512


def _softplus(x):
    return jnp.maximum(x, 0.0) + jnp.log(1.0 + jnp.exp(-jnp.abs(x)))


def _split3(x):
    hi = x.astype(BF16)
    r1 = x - hi.astype(F32)
    mid = r1.astype(BF16)
    lo = (r1 - mid.astype(F32)).astype(BF16)
    return hi, mid, lo


def _ssd_kernel(xbc_ref, z_ref, dt_ref, dtt_ref, wconv_ref, bconv_ref, dtb_ref, alog_ref,
                dtbt_ref, alogt_ref, dskip_ref, norm_ref, y_ref,
                tail_scr, xact_scr, state_scr):
    c = pl.program_id(0)
    L, P, N, W = SSD_CHUNK, SSD_HEAD_DIM, SSD_STATE, SSD_WIDTH
    assert P & (P - 1) == 0

    @pl.when(c == 0)
    def _():
        tail_scr[...] = jnp.zeros(tail_scr.shape, F32)
        state_scr[...] = jnp.zeros(state_scr.shape, F32)

    rr = lax.broadcasted_iota(jnp.int32, (L, L), 0)
    cc = lax.broadcasted_iota(jnp.int32, (L, L), 1)
    causal = rr >= cc

    shifts = [jnp.where(rr - cc == k, 1.0, 0.0).astype(BF16) for k in range(1, SSD_CONV)]
    row8 = lax.broadcasted_iota(jnp.int32, (SSD_TAIL, SSD_CONV_COLS), 0)
    for c0 in range(0, SSD_XBC, SSD_CONV_COLS):
        cols = slice(c0, c0 + SSD_CONV_COLS)
        x = xbc_ref[:, cols]
        tail = tail_scr[:, cols]
        acc = bconv_ref[:, cols] + wconv_ref[SSD_CONV - 1:SSD_CONV, cols] * x.astype(F32)
        fix = jnp.zeros((SSD_TAIL, SSD_CONV_COLS), F32)
        for k in range(1, SSD_CONV):
            wk = wconv_ref[SSD_CONV - 1 - k:SSD_CONV - k, cols]
            acc = acc + wk * _dot(shifts[k - 1], x)
            fix = fix + wk * jnp.where(row8 < k, pltpu.roll(tail, k, 0), 0.0)
        acc = jnp.concatenate([acc[:SSD_TAIL] + fix, acc[SSD_TAIL:]], axis=0)
        xact_scr[:, cols] = _silu(acc)
        tail_scr[:, cols] = x[L - SSD_TAIL:, :].astype(F32)

    dtv = _softplus(dt_ref[...] + dtb_ref[...])
    ad = dtv * (-jnp.exp(alog_ref[...]))
    dtt = _softplus(dtt_ref[...] + dtbt_ref[...])
    adt = dtt * (-jnp.exp(alogt_ref[...]))
    lower = jnp.where(causal, 1.0, 0.0).astype(BF16)
    upper = jnp.where(rr <= cc, 1.0, 0.0).astype(BF16)
    a_cs = sum(_dot(lower, t) for t in _split3(ad))
    a_cst = sum(_dot(t, upper) for t in _split3(adt))
    last = a_cs[L - 1:L, :]

    factors = jnp.concatenate([dtv, jnp.exp(a_cs), jnp.exp(last - a_cs)], axis=0).astype(BF16)
    group_w = W // SSD_GROUPS
    heads_per_group = SSD_HEADS // SSD_GROUPS
    e_row = lax.broadcasted_iota(jnp.int32, (LANES, group_w), 0)
    e_head = lax.shift_right_logical(lax.broadcasted_iota(jnp.int32, (LANES, group_w), 1),
                                     P.bit_length() - 1)
    pair_lane = lax.broadcasted_iota(jnp.int32, (L, 2 * P), 1)

    for g in range(SSD_GROUPS):
        gc = slice(g * group_w, (g + 1) * group_w)
        expand = jnp.where(e_row == g * heads_per_group + e_head, 1.0, 0.0).astype(BF16)
        spread = _dot(factors, expand)
        xs = xact_scr[:, gc]
        xd = xs * spread[:L]
        xd_b = xd.astype(BF16)
        xe_b = (xd * spread[2 * L:]).astype(BF16)
        bg = xact_scr[:, W + g * N:W + (g + 1) * N].astype(BF16)
        cg = xact_scr[:, W + SSD_BC + g * N:W + SSD_BC + (g + 1) * N].astype(BF16)
        st = state_scr[gc, :]
        y_off = _dot_nt(cg, st.astype(BF16)) * spread[L:2 * L]
        new = lax.dot_general(xe_b, bg, (((0,), (0,)), ((), ())), preferred_element_type=F32)
        cb = _dot_nt(cg, bg)
        pairs = []
        for r in range(0, heads_per_group, 2):
            h = g * heads_per_group + r
            xd_pair = xd_b[:, r * P:(r + 2) * P]
            halves = []
            for hh in (h, h + 1):
                dec = jnp.exp(jnp.where(causal, a_cs[:, hh:hh + 1] - a_cst[hh:hh + 1, :], -jnp.inf))
                halves.append(_dot((cb * dec).astype(BF16), xd_pair))
                hr = slice((hh - g * heads_per_group) * P, (hh - g * heads_per_group + 1) * P)
                state_scr[g * group_w + hr.start:g * group_w + hr.stop, :] = (
                    jnp.exp(a_cst[hh:hh + 1, L - 1:L]) * st[hr, :] + new[hr, :])
            pairs.append(jnp.where(pair_lane < P, halves[0], halves[1]))
        y = jnp.concatenate(pairs, axis=1) + y_off + xs * dskip_ref[:, gc]
        y = y * _silu(z_ref[:, gc].astype(F32))
        y = y * lax.rsqrt(jnp.mean(y * y, axis=-1, keepdims=True) + NORM_EPS)
        y_ref[:, gc] = (y * norm_ref[:, gc]).astype(y_ref.dtype)


def _ssd(xbc, z, dt, w_conv, b_conv, dt_bias, a_log, d_skip, norm):
    s = xbc.shape[0]
    L, nh = SSD_CHUNK, SSD_HEADS
    assert s % L == 0
    dtt = dt[:, :nh].T
    pad = lambda v: jnp.pad(v, (0, LANES - nh)).reshape(1, LANES)
    full = lambda r, cdim: pl.BlockSpec((r, cdim), lambda c: (0, 0))
    return pl.pallas_call(
        _ssd_kernel,
        grid=(s // L,),
        in_specs=[
            pl.BlockSpec((L, SSD_XBC), lambda c: (c, 0)),
            pl.BlockSpec((L, SSD_WIDTH), lambda c: (c, 0)),
            pl.BlockSpec((L, LANES), lambda c: (c, 0)),
            pl.BlockSpec((nh, L), lambda c: (0, c)),
            full(SSD_CONV, SSD_XBC), full(1, SSD_XBC),
            full(1, LANES), full(1, LANES), full(nh, 1), full(nh, 1),
            full(1, SSD_WIDTH), full(1, SSD_WIDTH),
        ],
        out_specs=pl.BlockSpec((L, SSD_WIDTH), lambda c: (c, 0)),
        out_shape=jax.ShapeDtypeStruct((s, SSD_WIDTH), BF16),
        scratch_shapes=[
            pltpu.VMEM((SSD_TAIL, SSD_XBC), F32),
            pltpu.VMEM((L, SSD_XBC), F32),
            pltpu.VMEM((SSD_WIDTH, SSD_STATE), F32),
        ],
        compiler_params=_cparams(("arbitrary",)),
        name="ssd_scan",
    )(xbc, z, dt, dtt, w_conv, b_conv.reshape(1, SSD_XBC), pad(dt_bias), pad(a_log),
      dt_bias.reshape(nh, 1), a_log.reshape(nh, 1),
      jnp.repeat(d_skip, SSD_HEAD_DIM).reshape(1, SSD_WIDTH), norm.reshape(1, SSD_WIDTH))


HALO = 16


def _merge_kernel(ya_ref, ys_ref, cb_ref, cc_ref, cx_ref, hc_ref, hx_ref, ga_ref, gs_ref, gc_ref,
                  wsc_ref, wa_ref, ws_ref, wc_ref, o_ref, ext_scr, yc_scr):
    i = pl.program_id(0)
    j = pl.program_id(1)
    tm = ya_ref.shape[0]

    @pl.when(j == 0)
    def _():
        halo = hc_ref[...].astype(F32) * hx_ref[...].astype(F32)
        ext_scr[0:HALO, :] = jnp.where(i > 0, halo, 0.0)
        for r0 in range(0, tm, ROW_CHUNK):
            rows = slice(r0, r0 + ROW_CHUNK)
            ext_scr[HALO + r0:HALO + r0 + ROW_CHUNK, :] = cc_ref[rows, :].astype(F32) * cx_ref[rows, :].astype(F32)
        for r0 in range(0, tm, ROW_CHUNK):
            rows = slice(r0, r0 + ROW_CHUNK)
            acc = jnp.zeros((ROW_CHUNK, CONV_WIDTH), F32)
            for k in range(CONV_K):
                start = HALO + r0 - (CONV_K - 1) + k
                acc = acc + wsc_ref[k:k + 1, :] * ext_scr[start:start + ROW_CHUNK, :]
            yc_scr[rows, :] = (cb_ref[rows, :].astype(F32) * acc).astype(BF16)

    merged = (_sigmoid(ga_ref[...].astype(F32)) * _dot(ya_ref[...], wa_ref[...].astype(BF16))
              + _sigmoid(gs_ref[...].astype(F32)) * _dot(ys_ref[...], ws_ref[...].astype(BF16))
              + _sigmoid(gc_ref[...].astype(F32)) * _dot(yc_scr[...], wc_ref[...].astype(BF16)))
    o_ref[...] = merged.astype(o_ref.dtype)


def _branch_merge(y_attn, y_ssd, tail, w_sc, w_a, w_s, w_c, lead, tm=1024, tn=512):
    s = y_attn.shape[0]
    d = w_a.shape[-1]
    tm = min(tm, s)
    cw = CONV_WIDTH
    g0 = 3 * cw // tn
    gd = d // tn
    row = lambda width, cb: pl.BlockSpec((tm, width), lambda i, j: (i, cb))
    halo = lambda cb: pl.BlockSpec((HALO, cw), lambda i, j: (jnp.maximum(i * (tm // HALO) - 1, 0), cb))
    gate = lambda k: pl.BlockSpec((tm, tn), lambda i, j: (i, g0 + k * gd + j))
    wcol = lambda kdim: _wspec(lead, (kdim, tn), lambda i, j: (0, j))
    return pl.pallas_call(
        _merge_kernel,
        grid=(s // tm, d // tn),
        in_specs=[
            row(ATTN_WIDTH, 0), row(SSD_WIDTH, 0),
            row(cw, 0), row(cw, 1), row(cw, 2), halo(1), halo(2),
            gate(0), gate(1), gate(2),
            pl.BlockSpec((CONV_K, cw), lambda i, j: (0, 0)),
            wcol(ATTN_WIDTH), wcol(SSD_WIDTH), wcol(cw),
        ],
        out_specs=pl.BlockSpec((tm, tn), lambda i, j: (i, j)),
        out_shape=jax.ShapeDtypeStruct((s, d), BF16),
        scratch_shapes=[pltpu.VMEM((HALO + tm, cw), F32), pltpu.VMEM((tm, cw), BF16)],
        compiler_params=_cparams(("arbitrary", "arbitrary")),
        name="branch_merge",
    )(y_attn, y_ssd, tail, tail, tail, tail, tail, tail, tail, tail, w_sc, w_a, w_s, w_c)


def _outproj_kernel(m_ref, w_ref, x_ref, gate_ref, o_ref, w_scr):
    @pl.when(pl.program_id(1) == 0)
    def _():
        w_scr[...] = w_ref[...].astype(BF16)

    o_ref[...] = x_ref[...] + gate_ref[...] * _dot(m_ref[...], w_scr[...])


def _out_proj(merged, w, lead, x, gate, tm=1024, tn=1024):
    s, d = x.shape
    tm = min(tm, s)
    return pl.pallas_call(
        _outproj_kernel,
        grid=(d // tn, s // tm),
        in_specs=[
            pl.BlockSpec((tm, merged.shape[1]), lambda j, i: (i, 0)),
            _wspec(lead, (merged.shape[1], tn), lambda j, i: (0, j)),
            pl.BlockSpec((tm, tn), lambda j, i: (i, j)),
            pl.BlockSpec((1, tn), lambda j, i: (0, j)),
        ],
        out_specs=pl.BlockSpec((tm, tn), lambda j, i: (i, j)),
        out_shape=jax.ShapeDtypeStruct((s, d), F32),
        scratch_shapes=[pltpu.VMEM((merged.shape[1], tn), BF16)],
        compiler_params=_cparams(("arbitrary", "arbitrary")),
        name="out_proj",
    )(merged, w, x, gate)


def _token_mix(x, gain, scale, shift, gate, l, w_mix_in, qk_norm, rel_bias, w_ssd_conv, b_ssd_conv,
               ssd_dt_bias, ssd_a_log, ssd_d, ssd_norm, w_sc_conv, w_br_attn, w_br_ssd, w_br_conv,
               w_mix_out):
    d = x.shape[1]
    lead = (l,)
    c_v = 2 * ATTN_WIDTH
    c_z = c_v + ATTN_WIDTH
    c_xbc = c_z + SSD_WIDTH
    c_dt = c_xbc + SSD_XBC
    c_tail = c_dt + SSD_HEADS
    n_tail = 3 * CONV_WIDTH + 3 * d
    assert w_mix_in.shape[-1] == c_tail + n_tail

    hgain = jnp.concatenate([jnp.tile(qk_norm[l, 0] * ATTN_Q_PRESCALE, ATTN_HEADS),
                             jnp.tile(qk_norm[l, 1], ATTN_HEADS)])[None]
    wt = jnp.swapaxes(w_mix_in, 1, 2)
    h = _norm(x, gain, scale, shift)
    qk = _proj(h, wt, lead, 0, 2 * ATTN_WIDTH, BF16, head_gain=hgain)
    vt = _proj(h, wt, lead, c_v, ATTN_WIDTH, BF16, transpose_out=True)
    z = _proj(h, wt, lead, c_z, SSD_WIDTH, BF16)
    xbc = _proj(h, wt, lead, c_xbc, SSD_XBC, BF16)
    wt_dt = jnp.pad(wt[l, c_dt:c_tail, :], ((0, LANES - SSD_HEADS), (0, 0)))
    dt = _proj(h, wt_dt, (), 0, LANES, F32)
    tail = _proj(h, wt, lead, c_tail, n_tail, BF16)

    y_attn = _moba_attention(qk, vt, rel_bias)
    y_ssd = _ssd(xbc, z, dt, w_ssd_conv[l], b_ssd_conv[l], ssd_dt_bias[l], ssd_a_log[l], ssd_d[l], ssd_norm[l])
    merged = _branch_merge(y_attn, y_ssd, tail, w_sc_conv[l], w_br_attn[l].astype(BF16),
                           w_br_ssd[l].astype(BF16), w_br_conv[l].astype(BF16), ())
    return _out_proj(merged, w_mix_out, lead, x, gate)


def kernel(x, c, w_ada, b_ada, norm_gain, w_ffn_in, w_ffn_out, w_mix_in, qk_norm, rel_bias, w_ssd_conv, b_ssd_conv, ssd_dt_bias, ssd_a_log, ssd_d, ssd_norm, w_sc_conv, w_br_attn, w_br_ssd, w_br_conv, w_mix_out):
    b, s, d = x.shape
    depth = w_ada.shape[0]
    assert b == 1 and s % math.lcm(MOBA_BLOCK, SSD_CHUNK) == 0
    xs = x.reshape(s, d)
    ada = _ada_proj(c, w_ada, b_ada).reshape(depth, N_SUBLAYERS, 3, 1, d)
    for l in range(depth):
        mod = lambda i: (norm_gain[l, i][None], ada[l, i, 1], ada[l, i, 0], ada[l, i, 2])
        xs = _ffn(xs, *mod(0), w_ffn_in, w_ffn_out, (l, 0))
        xs = _token_mix(xs, *mod(1), l, w_mix_in, qk_norm, rel_bias, w_ssd_conv, b_ssd_conv,
                        ssd_dt_bias, ssd_a_log, ssd_d, ssd_norm, w_sc_conv,
                        w_br_attn, w_br_ssd, w_br_conv, w_mix_out)
        xs = _ffn(xs, *mod(2), w_ffn_in, w_ffn_out, (l, 1))
    return xs.reshape(b, s, d)
```
